```python
import jax, jax.numpy as jnp
from jax import lax
import numpy as np

D_MODEL = 1024
BATCH = 2
SEQ = 8192
DEPTH = 1

RWKV_HEAD = 64
RWKV_HEADS = 8
RWKV_WIDTH = RWKV_HEADS * RWKV_HEAD
DECAY_RANK = 64
ICLR_RANK = 64
ATT_HEAD = 64
ATT_Q_HEADS = 8
ATT_KV_HEADS = 2
ATT_GROUP = ATT_Q_HEADS // ATT_KV_HEADS
ATT_WIDTH = ATT_Q_HEADS * ATT_HEAD
ATT_KV_WIDTH = ATT_KV_HEADS * ATT_HEAD
QKV_WIDTH = ATT_WIDTH + 2 * ATT_KV_WIDTH
WINDOW = 128
BLOCK = 128
N_BRANCHES = 2
SHIFT_WIDTH = 3 * RWKV_WIDTH + DECAY_RANK + ICLR_RANK
IN_COLS = SHIFT_WIDTH + RWKV_WIDTH + QKV_WIDTH + ATT_WIDTH + N_BRANCHES * D_MODEL
RMS_EPS = 1e-6
GN_EPS = 64e-5
L2_EPS = 1e-12
NEG_INF = -1e30

kernel_name = 'hybrid_rwkv7_swa_sink_gated'


def _split(t, sizes):
    out, off = [], 0
    for s in sizes:
        out.append(t[..., off:off + s])
        off += s
    return out


def rms_norm(x, g):
    xf = x.astype(jnp.float32)
    y = xf * lax.rsqrt(jnp.mean(xf * xf, axis=-1, keepdims=True) + RMS_EPS)
    return (y * g.astype(jnp.float32)).astype(x.dtype)


def token_shift(p, mu):
    prev = jnp.pad(p[:, :-1], ((0, 0), (1, 0), (0, 0)))
    return p + (prev - p) * mu


def rwkv7_time_mix(r, k, v, w_lo, a_lo, w0, w_decay_up, a0, w_iclr_up, k_k, k_a, r_k, gn_w, gn_b):
    B, T, _ = r.shape
    f32 = jnp.float32
    H, N = RWKV_HEADS, RWKV_HEAD
    w_log = -jax.nn.softplus(-(w0 + jnp.tanh(w_lo) @ w_decay_up)) - 0.5
    decay = jnp.exp(-jnp.exp(w_log.astype(f32)))
    a = jax.nn.sigmoid(a0 + a_lo @ w_iclr_up)
    kk = (k * k_k).reshape(B, T, H, N).astype(f32)
    kk = kk / jnp.maximum(jnp.sqrt(jnp.sum(kk * kk, axis=-1, keepdims=True)), L2_EPS)
    k = k * (1.0 + (a - 1.0) * k_a)
    rh = r.reshape(B, T, H, N).astype(f32)
    kh = k.reshape(B, T, H, N).astype(f32)
    vh = v.reshape(B, T, H, N).astype(f32)
    wh = decay.reshape(B, T, H, N)
    ah = a.reshape(B, T, H, N).astype(f32)
    tm = lambda t: jnp.moveaxis(t, 1, 0)

    def step(S, inp):
        r_t, w_t, k_t, v_t, kk_t, a_t = inp
        sa = jnp.einsum('bhvk,bhk->bhv', S, kk_t)
        S = (S * w_t[:, :, None, :]
             - sa[..., None] * (kk_t * a_t)[:, :, None, :]
             + v_t[..., None] * k_t[:, :, None, :])
        y_t = jnp.einsum('bhvk,bhk->bhv', S, r_t)
        return S, y_t

    S0 = jnp.zeros((B, H, N, N), f32)
    _, y = lax.scan(step, S0, (tm(rh), tm(wh), tm(kh), tm(vh), tm(kk), tm(ah)))
    y = jnp.moveaxis(y, 0, 1)
    mean = jnp.mean(y, axis=-1, keepdims=True)
    var = jnp.mean(jnp.square(y - mean), axis=-1, keepdims=True)
    y = ((y - mean) * lax.rsqrt(var + GN_EPS)).reshape(B, T, RWKV_WIDTH)
    y = y * gn_w.astype(f32) + gn_b.astype(f32)
    bonus = jnp.sum(rh * kh * r_k.reshape(H, N).astype(f32), axis=-1, keepdims=True) * vh
    return (y + bonus.reshape(B, T, RWKV_WIDTH)).astype(r.dtype)


def sliding_window_sink_attention(q, k, v, sinks):
    B, T, _ = q.shape
    nb = T // BLOCK
    f32 = jnp.float32
    KV, G, Dh = ATT_KV_HEADS, ATT_GROUP, ATT_HEAD
    qb = q.reshape(B, nb, BLOCK, KV, G, Dh)
    k = k.reshape(B, T, KV, Dh)
    v = v.reshape(B, T, KV, Dh)

    def banded(t):
        prev = jnp.pad(t, ((0, 0), (BLOCK, 0), (0, 0), (0, 0)))[:, :T]
        return jnp.concatenate([prev.reshape(B, nb, BLOCK, KV, Dh),
                                t.reshape(B, nb, BLOCK, KV, Dh)], axis=2)

    kb, vb = banded(k), banded(v)
    s = jnp.einsum('bnqhgd,bnkhd->bnhgqk', qb, kb).astype(f32) * (Dh ** -0.5)
    qi = jnp.arange(BLOCK)[:, None]
    kj = jnp.arange(2 * BLOCK)[None, :]
    dist = qi + BLOCK - kj
    band = (dist >= 0) & (dist < WINDOW)
    valid = band[None] & ((jnp.arange(nb)[:, None, None] > 0) | (kj[None] >= BLOCK))
    s = jnp.where(valid[None, :, None, None], s, NEG_INF)
    sink = jnp.broadcast_to(sinks.astype(f32).reshape(1, 1, KV, G, 1, 1), s.shape[:-1] + (1,))
    p = jax.nn.softmax(jnp.concatenate([s, sink], axis=-1), axis=-1)[..., :-1]
    o = jnp.einsum('bnhgqk,bnkhd->bnqhgd', p.astype(v.dtype), vb)
    return o.reshape(B, T, ATT_WIDTH)


def hybrid_layer(x, g_pre, w_in, mu_shift, w0, w_decay_up, a0, w_iclr_up, k_k, k_a, r_k,
                 gn_w, gn_b, b_qkv, sinks, w_branch_rwkv, w_branch_att, w_out):
    h = rms_norm(x, g_pre)
    p = jnp.einsum('btd,dc->btc', h, w_in)
    shifted, g_rwkv, qkv, g_att, gates = _split(
        p, (SHIFT_WIDTH, RWKV_WIDTH, QKV_WIDTH, ATT_WIDTH, N_BRANCHES * D_MODEL))
    shifted = token_shift(shifted, mu_shift)
    r, k, v, w_lo, a_lo = _split(shifted, (RWKV_WIDTH, RWKV_WIDTH, RWKV_WIDTH, DECAY_RANK, ICLR_RANK))
    y_rwkv = rwkv7_time_mix(r, k, v, w_lo, a_lo, w0, w_decay_up, a0, w_iclr_up,
                            k_k, k_a, r_k, gn_w, gn_b)
    q, k_att, v_att = _split(qkv + b_qkv, (ATT_WIDTH, ATT_KV_WIDTH, ATT_KV_WIDTH))
    y_att = sliding_window_sink_attention(q, k_att, v_att, sinks)
    br_rwkv = (y_rwkv * jax.nn.silu(g_rwkv)) @ w_branch_rwkv
    br_att = (y_att * jax.nn.silu(g_att)) @ w_branch_att
    gate_rwkv, gate_att = _split(jax.nn.sigmoid(gates), (D_MODEL, D_MODEL))
    merged = gate_rwkv * br_rwkv + gate_att * br_att
    return x + merged @ w_out


def setup_inputs(seed: int = 0) -> dict:
    key = jax.random.key(seed)
    ks = jax.random.split(key, 20)
    L = DEPTH
    f32 = jnp.float32
    nrm = lambda k, shape, scale: jax.random.normal(k, shape, f32) * scale
    return {
        'x': nrm(ks[0], (BATCH, SEQ, D_MODEL), 1.0),
        'g_pre': 1.0 + nrm(ks[1], (L, D_MODEL), 0.05),
        'w_in': nrm(ks[2], (L, D_MODEL, IN_COLS), D_MODEL ** -0.5),
        'mu_shift': jax.random.uniform(ks[3], (L, SHIFT_WIDTH), f32, 0.1, 0.9),
        'w0': jax.random.uniform(ks[4], (L, RWKV_WIDTH), f32, -6.0, 1.0),
        'w_decay_up': nrm(ks[5], (L, DECAY_RANK, RWKV_WIDTH), 0.5 * DECAY_RANK ** -0.5),
        'a0': nrm(ks[6], (L, RWKV_WIDTH), 0.5),
        'w_iclr_up': nrm(ks[7], (L, ICLR_RANK, RWKV_WIDTH), 0.5 * ICLR_RANK ** -0.5),
        'k_k': 0.85 + nrm(ks[8], (L, RWKV_WIDTH), 0.05),
        'k_a': 1.0 + nrm(ks[9], (L, RWKV_WIDTH), 0.05),
        'r_k': nrm(ks[10], (L, RWKV_WIDTH), 0.1),
        'gn_w': 1.0 + nrm(ks[11], (L, RWKV_WIDTH), 0.05),
        'gn_b': nrm(ks[12], (L, RWKV_WIDTH), 0.02),
        'b_qkv': nrm(ks[13], (L, QKV_WIDTH), 0.02),
        'sinks': nrm(ks[14], (L, ATT_Q_HEADS), 1.0),
        'w_branch_rwkv': nrm(ks[15], (L, RWKV_WIDTH, D_MODEL), RWKV_WIDTH ** -0.5),
        'w_branch_att': nrm(ks[16], (L, ATT_WIDTH, D_MODEL), ATT_WIDTH ** -0.5),
        'w_out': nrm(ks[17], (L, D_MODEL, D_MODEL), D_MODEL ** -0.5),
        'g_final': 1.0 + nrm(ks[18], (D_MODEL,), 0.05),
    }


def reference(x, g_pre, w_in, mu_shift, w0, w_decay_up, a0, w_iclr_up, k_k, k_a, r_k,
              gn_w, gn_b, b_qkv, sinks, w_branch_rwkv, w_branch_att, w_out, g_final):
    for l in range(DEPTH):
        x = hybrid_layer(x, g_pre[l], w_in[l], mu_shift[l], w0[l], w_decay_up[l], a0[l],
                         w_iclr_up[l], k_k[l], k_a[l], r_k[l], gn_w[l], gn_b[l], b_qkv[l],
                         sinks[l], w_branch_rwkv[l], w_branch_att[l], w_out[l])
    return rms_norm(x, g_final)
```

```python
import functools
import math

import jax
import jax.numpy as jnp
from jax import lax
from jax.experimental import pallas as pl
from jax.experimental.pallas import tpu as pltpu

D_MODEL = 1024
HEAD = 64
RWKV_WIDTH = 512
LORA = 64
ATT_WIDTH = 512
ATT_KV_WIDTH = 128
ATT_GROUP = 4
ATT_BLOCK = 128
SHIFT_WIDTH = 3 * RWKV_WIDTH + 2 * LORA
COL_G_RWKV = SHIFT_WIDTH
COL_QKV = COL_G_RWKV + RWKV_WIDTH
COL_G_ATT = COL_QKV + ATT_WIDTH + 2 * ATT_KV_WIDTH
COL_GATES = COL_G_ATT + ATT_WIDTH
IN_COLS = COL_GATES + 2 * D_MODEL
RMS_EPS = 1e-6
GN_EPS = 64e-5
L2_EPS = 1e-12
NEG_INF = -1e30

LANES = 128
TOKENS_PER_STEP = 256
CHUNK = 64
SQUARINGS = int(math.log2(CHUNK))
VMEM_LIMIT_BYTES = 56 * 1024 * 1024

F32 = jnp.float32
BF16 = jnp.bfloat16


def _mm(a, b, *, tb=False, passes=1):
    dn = (((1,), (1 if tb else 0,)), ((), ()))
    dot = lambda x, y: lax.dot_general(x, y, dn, preferred_element_type=F32)
    a_hi = a.astype(BF16)
    b_hi = b.astype(BF16)
    out = dot(a_hi, b_hi)
    if passes >= 2:
        a_lo = (a - a_hi.astype(F32)).astype(BF16)
        out = out + dot(a_lo, b_hi)
    if passes >= 3:
        b_lo = (b - b_hi.astype(F32)).astype(BF16)
        out = out + dot(a_hi, b_lo)
    return out


def _sigmoid(x):
    return 1.0 / (1.0 + jnp.exp(-x))


def _rms_norm(x, g):
    return x * lax.rsqrt(jnp.mean(x * x, axis=-1, keepdims=True) + RMS_EPS) * g


def _iota(shape, dim):
    return lax.broadcasted_iota(jnp.int32, shape, dim)


def _block_diag(z, mask):
    return jnp.where(mask, jnp.concatenate([z, z], axis=0), 0.0)


def _rwkv_chunk_pair(rt, bt, at, kt, ah, kh, v, ptot, h_bd):
    row = _iota((CHUNK, LANES), 0)
    col = _iota((CHUNK, LANES), 1)
    src = col & (HEAD - 1)
    strict = src < row
    incl = src <= row
    eye = src == row
    lane_head0 = col < HEAD
    r2 = _iota((LANES, LANES), 0)
    c2 = _iota((LANES, LANES), 1)
    bd_mask = (r2 >= HEAD) == (c2 >= HEAD)
    bd = functools.partial(_block_diag, mask=bd_mask)

    lhs = jnp.concatenate([bt, rt], axis=0)
    rhs_t = jnp.concatenate([jnp.where(lane_head0, at, 0.0), jnp.where(lane_head0, 0.0, at),
                             jnp.where(lane_head0, kt, 0.0), jnp.where(lane_head0, 0.0, kt)], axis=0)
    a_all = _mm(lhs, rhs_t, tb=True)
    a_ba = jnp.where(strict, a_all[:CHUNK, :LANES], 0.0)
    a_bk = jnp.where(strict, a_all[:CHUNK, LANES:], 0.0)
    a_ra = jnp.where(incl, a_all[CHUNK:, :LANES], 0.0)
    a_rk = jnp.where(incl, a_all[CHUNK:, LANES:], 0.0)

    x = jnp.where(eye, 1.0, 0.0) + a_ba
    p = _mm(a_ba, bd(a_ba))
    for level in range(2, SQUARINGS + 1):
        if level < SQUARINGS:
            prod = _mm(jnp.concatenate([x, p], axis=0), bd(p))
            x = x + prod[:CHUNK]
            p = prod[CHUNK:]
        else:
            x = x + _mm(x, bd(p))

    av = _mm(jnp.concatenate([a_bk, a_rk], axis=0), bd(v))
    abv, arkv = av[:CHUNK], av[CHUNK:]
    gw = _mm(x, jnp.concatenate([bd(bt), bd(abv)], axis=1))
    g, w2 = gw[:, :LANES], gw[:, LANES:]
    rg = _mm(a_ra, jnp.concatenate([bd(g), bd(w2)], axis=1))
    r_hat = rt + rg[:, :LANES]
    y0 = rg[:, LANES:] + arkv

    z_t = jnp.concatenate([ah, kh], axis=0).T
    rhs = jnp.concatenate([jnp.concatenate([g, w2], axis=1),
                           jnp.concatenate([jnp.zeros_like(v), v], axis=1)], axis=0)
    mn = _mm(z_t, rhs)
    m_bd = jnp.where(bd_mask, mn[:, :LANES], 0.0) + jnp.where(r2 == c2, ptot, 0.0)
    n_bd = jnp.where(bd_mask, mn[:, LANES:], 0.0)

    out = _mm(jnp.concatenate([r_hat, m_bd], axis=0), h_bd)
    return out[:CHUNK] + y0, out[CHUNK:] + n_bd


def _layer_kernel(sinks_ref, x_ref, g_pre_ref, w_in_ref, mu_ref, w0_ref, w_lora_ref, a0_ref, k_k_ref,
                  k_a_ref, r_k_ref, gn_w_ref, gn_b_ref, b_qkv_ref, w_br_rwkv_ref, w_br_att_ref,
                  w_out_ref, g_final_ref, head_ones_ref, tri_ref, out_ref,
                  state_ref, tail_ref, k_prev_ref, v_prev_ref):
    tb = TOKENS_PER_STEP
    t_blk = pl.program_id(1)

    @pl.when(t_blk == 0)
    def _():
        state_ref[...] = jnp.zeros_like(state_ref)
        tail_ref[...] = jnp.zeros_like(tail_ref)
        k_prev_ref[...] = jnp.zeros_like(k_prev_ref)
        v_prev_ref[...] = jnp.zeros_like(v_prev_ref)

    x = x_ref[0]
    h = _rms_norm(x, g_pre_ref[...]).astype(BF16)
    proj = lambda lo, hi: jnp.dot(h, w_in_ref[:, lo:hi], preferred_element_type=F32)

    p_shift = proj(0, SHIFT_WIDTH)
    prev = pltpu.roll(p_shift, 1, 0)
    prev = jnp.where(_iota(p_shift.shape, 0) == 0, tail_ref[7:8, :], prev)
    tail_ref[...] = p_shift[tb - 8:, :]
    shifted = p_shift + (prev - p_shift) * mu_ref[...]
    r = shifted[:, 0:RWKV_WIDTH]
    k = shifted[:, RWKV_WIDTH:2 * RWKV_WIDTH]
    v = shifted[:, 2 * RWKV_WIDTH:3 * RWKV_WIDTH]
    lo = shifted[:, 3 * RWKV_WIDTH:]
    lo = jnp.where(_iota(lo.shape, 1) < LORA, jnp.tanh(lo), lo)
    lora = _mm(lo, w_lora_ref[...])

    head_ones = head_ones_ref[...]
    head_sum = lambda z: _mm(z, head_ones, passes=2)
    log_w = -math.exp(-0.5) * _sigmoid(w0_ref[...] + lora[:, :RWKV_WIDTH])
    a = _sigmoid(a0_ref[...] + lora[:, RWKV_WIDTH:])
    kk = k * k_k_ref[...]
    kk = kk / jnp.maximum(jnp.sqrt(head_sum(kk * kk)), L2_EPS)
    k = k * (1.0 + (a - 1.0) * k_a_ref[...])
    alpha = -(a * kk)

    n_pairs = RWKV_WIDTH // LANES
    ys = []
    for c in range(tb // CHUNK):
        rows = slice(c * CHUNK, (c + 1) * CHUNK)
        lw = log_w[rows]
        lw_hi = lw.astype(BF16)
        lw_mid = (lw - lw_hi.astype(F32)).astype(BF16)
        lw_lo = (lw - lw_hi.astype(F32) - lw_mid.astype(F32)).astype(BF16)
        tri = tri_ref[...]
        cum = (jnp.dot(tri, lw_hi, preferred_element_type=F32)
               + jnp.dot(tri, lw_mid, preferred_element_type=F32)
               + jnp.dot(tri, lw_lo, preferred_element_type=F32))
        total = cum[CHUNK - 1:CHUNK, :]
        p_incl = jnp.exp(cum)
        p_inv = jnp.exp(-cum)
        p_tail = jnp.exp(total - cum)
        rt = r[rows] * p_incl
        bt = kk[rows] * jnp.exp(cum - lw)
        at = alpha[rows] * p_inv
        kt = k[rows] * p_inv
        ah = alpha[rows] * p_tail
        kh = k[rows] * p_tail
        ptot = jnp.exp(total)
        y_pairs = []
        for pr in range(n_pairs):
            ln = slice(pr * LANES, (pr + 1) * LANES)
            y_pair, new_state = _rwkv_chunk_pair(rt[:, ln], bt[:, ln], at[:, ln], kt[:, ln], ah[:, ln],
                                                 kh[:, ln], v[rows, ln], ptot[:, ln], state_ref[pr])
            state_ref[pr] = new_state
            y_pairs.append(y_pair)
        ys.append(jnp.concatenate(y_pairs, axis=1))
    y = jnp.concatenate(ys, axis=0)

    mean = head_sum(y) * (1.0 / HEAD)
    yc = y - mean
    var = head_sum(yc * yc) * (1.0 / HEAD)
    y = yc * lax.rsqrt(var + GN_EPS) * gn_w_ref[...] + gn_b_ref[...]
    y_rwkv = y + head_sum(r * k * r_k_ref[...]) * v

    qkv = proj(COL_QKV, COL_G_ATT) + b_qkv_ref[...]
    q = qkv[:, :ATT_WIDTH] * (HEAD ** -0.5)
    k_att = qkv[:, ATT_WIDTH:ATT_WIDTH + ATT_KV_WIDTH]
    v_att = qkv[:, ATT_WIDTH + ATT_KV_WIDTH:]
    qi = _iota((ATT_BLOCK, 2 * ATT_BLOCK), 0)
    kj = _iota((ATT_BLOCK, 2 * ATT_BLOCK), 1)
    band = (kj > qi) & (kj <= qi + ATT_BLOCK)
    kv_lane0 = _iota((2 * ATT_BLOCK, LANES), 1) < HEAD
    out_lane0 = _iota((ATT_BLOCK, LANES), 1) < HEAD
    att_blocks = []
    k_prev = k_prev_ref[...]
    v_prev = v_prev_ref[...]
    for blk in range(tb // ATT_BLOCK):
        rows = slice(blk * ATT_BLOCK, (blk + 1) * ATT_BLOCK)
        k_cat = jnp.concatenate([k_prev, k_att[rows]], axis=0)
        v_cat = jnp.concatenate([v_prev, v_att[rows]], axis=0)
        k_swap = pltpu.roll(k_cat, HEAD, 1)
        v_swap = pltpu.roll(v_cat, HEAD, 1)
        if blk == 0:
            valid = band & ((kj >= ATT_BLOCK) | (t_blk > 0))
        else:
            valid = band
        outs, sink_terms = [], []
        for hd in range(ATT_WIDTH // HEAD):
            kv_head, parity, pair = hd // ATT_GROUP, hd % 2, hd // 2
            own_lanes = kv_lane0 if parity == 0 else ~kv_lane0
            k_h = jnp.where(own_lanes, k_cat if kv_head == parity else k_swap, 0.0)
            v_h = jnp.where(own_lanes, v_cat if kv_head == parity else v_swap, 1.0)
            s = _mm(q[rows, pair * LANES:(pair + 1) * LANES], k_h, tb=True)
            s = jnp.where(valid, s, NEG_INF)
            sink = sinks_ref[hd]
            m = jnp.maximum(jnp.max(s, axis=-1, keepdims=True), sink)
            outs.append(_mm(jnp.exp(s - m), v_h))
            sink_terms.append(jnp.exp(sink - m))
        pair_slabs = []
        for pair in range(ATT_WIDTH // LANES):
            o0, o1 = outs[2 * pair], outs[2 * pair + 1]
            num = jnp.where(out_lane0, o0, o1)
            den = pltpu.roll(jnp.where(out_lane0, o1, o0), HEAD, 1)
            den = den + jnp.where(out_lane0, sink_terms[2 * pair], sink_terms[2 * pair + 1])
            pair_slabs.append(num / den)
        att_blocks.append(jnp.concatenate(pair_slabs, axis=1))
        k_prev, v_prev = k_att[rows], v_att[rows]
    k_prev_ref[...] = k_prev
    v_prev_ref[...] = v_prev
    y_att = jnp.concatenate(att_blocks, axis=0)

    g_rwkv = proj(COL_G_RWKV, COL_QKV)
    g_att = proj(COL_G_ATT, COL_GATES)
    br_rwkv = _mm(y_rwkv * (g_rwkv * _sigmoid(g_rwkv)), w_br_rwkv_ref[...])
    br_att = _mm(y_att * (g_att * _sigmoid(g_att)), w_br_att_ref[...])
    merged = (_sigmoid(proj(COL_GATES, COL_GATES + D_MODEL)) * br_rwkv
              + _sigmoid(proj(COL_GATES + D_MODEL, IN_COLS)) * br_att)
    resid = x + _mm(merged, w_out_ref[...])
    out_ref[0] = _rms_norm(resid, g_final_ref[...])


def _const_spec(shape):
    return pl.BlockSpec(shape, lambda b, t: (0,) * len(shape), pipeline_mode=pl.Buffered(1))


def kernel(x, g_pre, w_in, mu_shift, w0, w_decay_up, a0, w_iclr_up, k_k, k_a, r_k, gn_w, gn_b, b_qkv,
           sinks, w_branch_rwkv, w_branch_att, w_out, g_final):
    assert g_pre.shape[0] == 1, "single layer"
    batch, seq, d = x.shape
    assert d == D_MODEL and seq % TOKENS_PER_STEP == 0 and w_in.shape[-1] == IN_COLS
    row = lambda p: p.reshape(1, -1).astype(F32)
    w_lora = jnp.zeros((2 * LORA, 2 * RWKV_WIDTH), F32)
    w_lora = w_lora.at[:LORA, :RWKV_WIDTH].set(w_decay_up[0]).at[LORA:, RWKV_WIDTH:].set(w_iclr_up[0])
    lane_head = jnp.arange(RWKV_WIDTH) // HEAD
    head_ones = (lane_head[:, None] == lane_head[None, :]).astype(BF16)
    tri = (jnp.arange(CHUNK)[:, None] >= jnp.arange(CHUNK)[None, :]).astype(BF16)
    operands = [
        x, row(g_pre[0]), w_in[0].astype(BF16), row(mu_shift[0]), row(w0[0]), w_lora.astype(BF16), row(a0[0]),
        row(k_k[0]), row(k_a[0]), row(r_k[0]), row(gn_w[0]), row(gn_b[0]), row(b_qkv[0]),
        w_branch_rwkv[0].astype(BF16), w_branch_att[0].astype(BF16), w_out[0].astype(BF16), row(g_final),
        head_ones, tri,
    ]
    x_spec = pl.BlockSpec((1, TOKENS_PER_STEP, D_MODEL), lambda b, t: (b, t, 0))
    in_specs = [pl.BlockSpec(memory_space=pltpu.SMEM), x_spec] + [_const_spec(op.shape) for op in operands[1:]]
    return pl.pallas_call(
        _layer_kernel,
        grid=(batch, seq // TOKENS_PER_STEP),
        in_specs=in_specs,
        out_specs=x_spec,
        out_shape=jax.ShapeDtypeStruct(x.shape, x.dtype),
        scratch_shapes=[
            pltpu.VMEM((RWKV_WIDTH // LANES, LANES, LANES), F32),
            pltpu.VMEM((8, SHIFT_WIDTH), F32),
            pltpu.VMEM((ATT_BLOCK, ATT_KV_WIDTH), F32),
            pltpu.VMEM((ATT_BLOCK, ATT_KV_WIDTH), F32),
        ],
        compiler_params=pltpu.CompilerParams(
            dimension_semantics=("arbitrary", "arbitrary"),
            vmem_limit_bytes=VMEM_LIMIT_BYTES,
        ),
        name="hybrid_rwkv7_swa_layer",
    )(sinks[0].astype(F32), *operands)
```

```python
import functools
import math

import jax
import jax.numpy as jnp
from jax import lax
from jax.experimental import pallas as pl
from jax.experimental.pallas import tpu as pltpu

D_MODEL = 1024
HEAD = 64
RWKV_WIDTH = 512
LORA = 64
ATT_WIDTH = 512
ATT_KV_WIDTH = 128
ATT_GROUP = 4
ATT_BLOCK = 128
SHIFT_WIDTH = 3 * RWKV_WIDTH + 2 * LORA
COL_G_RWKV = SHIFT_WIDTH
COL_QKV = COL_G_RWKV + RWKV_WIDTH
COL_G_ATT = COL_QKV + ATT_WIDTH + 2 * ATT_KV_WIDTH
COL_GATES = COL_G_ATT + ATT_WIDTH
IN_COLS = COL_GATES + 2 * D_MODEL
RMS_EPS = 1e-6
GN_EPS = 64e-5
L2_EPS = 1e-12
NEG_INF = -1e30

LANES = 128
TOKENS_PER_STEP = 256
CHUNK = 64
SQUARINGS = int(math.log2(CHUNK))
VMEM_LIMIT_BYTES = 56 * 1024 * 1024

F32 = jnp.float32
BF16 = jnp.bfloat16


def _mm(a, b, *, tb=False, passes=1):
    dn = (((1,), (1 if tb else 0,)), ((), ()))
    dot = lambda x, y: lax.dot_general(x, y, dn, preferred_element_type=F32)
    a_hi = a.astype(BF16)
    b_hi = b.astype(BF16)
    out = dot(a_hi, b_hi)
    if passes >= 2:
        a_lo = (a - a_hi.astype(F32)).astype(BF16)
        out = out + dot(a_lo, b_hi)
    if passes >= 3:
        b_lo = (b - b_hi.astype(F32)).astype(BF16)
        out = out + dot(a_hi, b_lo)
    return out


def _sigmoid(x):
    return 1.0 / (1.0 + jnp.exp(-x))


def _rms_norm(x, g):
    return x * lax.rsqrt(jnp.mean(x * x, axis=-1, keepdims=True) + RMS_EPS) * g


def _iota(shape, dim):
    return lax.broadcasted_iota(jnp.int32, shape, dim)


def _block_diag(z, mask):
    return jnp.where(mask, jnp.concatenate([z, z], axis=0), 0.0)


def _rwkv_chunks(slabs, ptots, n_pairs, state_ref):
    row = _iota((CHUNK, LANES), 0)
    col = _iota((CHUNK, LANES), 1)
    src = col & (HEAD - 1)
    strict = src < row
    incl = src <= row
    eye = src == row
    lane_head0 = col < HEAD
    r2 = _iota((LANES, LANES), 0)
    c2 = _iota((LANES, LANES), 1)
    bd_mask = (r2 >= HEAD) == (c2 >= HEAD)
    bd = functools.partial(_block_diag, mask=bd_mask)
    cat = jnp.concatenate
    n = len(slabs)

    a_all = []
    for rt, bt, at, kt, _, _, _ in slabs:
        rhs_t = cat([jnp.where(lane_head0, at, 0.0), jnp.where(lane_head0, 0.0, at),
                     jnp.where(lane_head0, kt, 0.0), jnp.where(lane_head0, 0.0, kt)], axis=0)
        a_all.append(_mm(cat([bt, rt], axis=0), rhs_t, tb=True))
    a_ba = [jnp.where(strict, a[:CHUNK, :LANES], 0.0) for a in a_all]
    a_bk = [jnp.where(strict, a[:CHUNK, LANES:], 0.0) for a in a_all]
    a_ra = [jnp.where(incl, a[CHUNK:, :LANES], 0.0) for a in a_all]
    a_rk = [jnp.where(incl, a[CHUNK:, LANES:], 0.0) for a in a_all]

    x = [jnp.where(eye, 1.0, 0.0) + a for a in a_ba]
    p = [_mm(a, bd(a)) for a in a_ba]
    for level in range(2, SQUARINGS + 1):
        if level < SQUARINGS:
            prod = [_mm(cat([x[i], p[i]], axis=0), bd(p[i])) for i in range(n)]
            x = [x[i] + prod[i][:CHUNK] for i in range(n)]
            p = [prod[i][CHUNK:] for i in range(n)]
        else:
            x = [x[i] + _mm(x[i], bd(p[i])) for i in range(n)]

    av = [_mm(cat([a_bk[i], a_rk[i]], axis=0), bd(slabs[i][6])) for i in range(n)]
    gw = [_mm(x[i], cat([bd(slabs[i][1]), bd(av[i][:CHUNK])], axis=1)) for i in range(n)]
    rg = [_mm(a_ra[i], cat([bd(gw[i][:, :LANES]), bd(gw[i][:, LANES:])], axis=1)) for i in range(n)]
    r_hat = [slabs[i][0] + rg[i][:, :LANES] for i in range(n)]
    y0 = [rg[i][:, LANES:] + av[i][CHUNK:] for i in range(n)]

    mn = []
    for i in range(n):
        _, _, _, _, ah, kh, v = slabs[i]
        z_t = cat([ah, kh], axis=0).T
        rhs = cat([gw[i], cat([jnp.zeros_like(v), v], axis=1)], axis=0)
        mn.append(_mm(z_t, rhs))
    m_bd = [jnp.where(bd_mask, mn[i][:, :LANES], 0.0) + jnp.where(r2 == c2, ptots[i], 0.0) for i in range(n)]
    n_bd = [jnp.where(bd_mask, mn[i][:, LANES:], 0.0) for i in range(n)]

    state = [state_ref[pr] for pr in range(n_pairs)]
    ys = []
    for i in range(n):
        pr = i % n_pairs
        out = _mm(cat([r_hat[i], m_bd[i]], axis=0), state[pr])
        ys.append(out[:CHUNK] + y0[i])
        state[pr] = out[CHUNK:] + n_bd[i]
    for pr in range(n_pairs):
        state_ref[pr] = state[pr]
    return ys


def _layer_kernel(sinks_ref, x_ref, g_pre_ref, w_in_ref, mu_ref, w0_ref, w_lora_ref, a0_ref, k_k_ref,
                  k_a_ref, r_k_ref, gn_w_ref, gn_b_ref, b_qkv_ref, w_br_rwkv_ref, w_br_att_ref,
                  w_out_ref, g_final_ref, head_ones_ref, tri_ref, out_ref,
                  state_ref, tail_ref, k_prev_ref, v_prev_ref):
    tb = TOKENS_PER_STEP
    t_blk = pl.program_id(1)

    @pl.when(t_blk == 0)
    def _():
        state_ref[...] = jnp.zeros_like(state_ref)
        tail_ref[...] = jnp.zeros_like(tail_ref)
        k_prev_ref[...] = jnp.zeros_like(k_prev_ref)
        v_prev_ref[...] = jnp.zeros_like(v_prev_ref)

    x = x_ref[0]
    h = _rms_norm(x, g_pre_ref[...]).astype(BF16)
    proj = lambda lo, hi: jnp.dot(h, w_in_ref[:, lo:hi], preferred_element_type=F32)
    cat = jnp.concatenate

    p_shift = proj(0, SHIFT_WIDTH)
    qkv = proj(COL_QKV, COL_G_ATT) + b_qkv_ref[...]
    g_rwkv = proj(COL_G_RWKV, COL_QKV)
    g_att = proj(COL_G_ATT, COL_GATES)
    gate_rwkv = proj(COL_GATES, COL_GATES + D_MODEL)
    gate_att = proj(COL_GATES + D_MODEL, IN_COLS)

    q = qkv[:, :ATT_WIDTH] * (HEAD ** -0.5)
    k_att = qkv[:, ATT_WIDTH:ATT_WIDTH + ATT_KV_WIDTH]
    v_att = qkv[:, ATT_WIDTH + ATT_KV_WIDTH:]
    qi = _iota((ATT_BLOCK, 2 * ATT_BLOCK), 0)
    kj = _iota((ATT_BLOCK, 2 * ATT_BLOCK), 1)
    band = (kj > qi) & (kj <= qi + ATT_BLOCK)
    kv_lane0 = _iota((2 * ATT_BLOCK, LANES), 1) < HEAD
    out_lane0 = _iota((ATT_BLOCK, LANES), 1) < HEAD
    n_att_heads = ATT_WIDTH // HEAD
    n_att_blocks = tb // ATT_BLOCK
    k_prev = k_prev_ref[...]
    v_prev = v_prev_ref[...]
    scores, v_ops = [], []
    for blk in range(n_att_blocks):
        rows = slice(blk * ATT_BLOCK, (blk + 1) * ATT_BLOCK)
        k_cat = cat([k_prev, k_att[rows]], axis=0)
        v_cat = cat([v_prev, v_att[rows]], axis=0)
        k_swap = pltpu.roll(k_cat, HEAD, 1)
        v_swap = pltpu.roll(v_cat, HEAD, 1)
        k_ops, v_blk = {}, {}
        for kv_head in range(ATT_KV_WIDTH // HEAD):
            for parity in range(2):
                own_lanes = kv_lane0 if parity == 0 else ~kv_lane0
                k_ops[kv_head, parity] = jnp.where(own_lanes, k_cat if kv_head == parity else k_swap, 0.0)
                v_blk[kv_head, parity] = jnp.where(own_lanes, v_cat if kv_head == parity else v_swap, 1.0)
        for hd in range(n_att_heads):
            kv_head, parity, pair = hd // ATT_GROUP, hd % 2, hd // 2
            scores.append(_mm(q[rows, pair * LANES:(pair + 1) * LANES], k_ops[kv_head, parity], tb=True))
            v_ops.append(v_blk[kv_head, parity])
        k_prev, v_prev = k_att[rows], v_att[rows]
    k_prev_ref[...] = k_prev
    v_prev_ref[...] = v_prev

    prev = pltpu.roll(p_shift, 1, 0)
    prev = jnp.where(_iota(p_shift.shape, 0) == 0, tail_ref[7:8, :], prev)
    tail_ref[...] = p_shift[tb - 8:, :]
    shifted = p_shift + (prev - p_shift) * mu_ref[...]
    r = shifted[:, 0:RWKV_WIDTH]
    k = shifted[:, RWKV_WIDTH:2 * RWKV_WIDTH]
    v = shifted[:, 2 * RWKV_WIDTH:3 * RWKV_WIDTH]
    lo = shifted[:, 3 * RWKV_WIDTH:]
    lo = jnp.where(_iota(lo.shape, 1) < LORA, jnp.tanh(lo), lo)
    lora = _mm(lo, w_lora_ref[...])

    head_ones = head_ones_ref[...]
    head_sum = lambda z: _mm(z, head_ones, passes=2)
    log_w = -math.exp(-0.5) * _sigmoid(w0_ref[...] + lora[:, :RWKV_WIDTH])
    a = _sigmoid(a0_ref[...] + lora[:, RWKV_WIDTH:])
    kk = k * k_k_ref[...]
    kk_norm2 = head_sum(kk * kk)
    k = k * (1.0 + (a - 1.0) * k_a_ref[...])
    bonus = head_sum(r * k * r_k_ref[...])
    kk = kk / jnp.maximum(jnp.sqrt(kk_norm2), L2_EPS)
    alpha = -(a * kk)

    n_pairs = RWKV_WIDTH // LANES
    n_chunks = tb // CHUNK
    tri = tri_ref[...]
    cums = []
    for c in range(n_chunks):
        lw = log_w[c * CHUNK:(c + 1) * CHUNK]
        lw_hi = lw.astype(BF16)
        lw_mid = (lw - lw_hi.astype(F32)).astype(BF16)
        lw_lo = (lw - lw_hi.astype(F32) - lw_mid.astype(F32)).astype(BF16)
        cums.append(jnp.dot(tri, lw_hi, preferred_element_type=F32)
                    + jnp.dot(tri, lw_mid, preferred_element_type=F32)
                    + jnp.dot(tri, lw_lo, preferred_element_type=F32))
    slabs, ptots = [], []
    for c in range(n_chunks):
        rows = slice(c * CHUNK, (c + 1) * CHUNK)
        cum = cums[c]
        total = cum[CHUNK - 1:CHUNK, :]
        p_incl = jnp.exp(cum)
        p_inv = jnp.exp(-cum)
        p_tail = jnp.exp(total - cum)
        full = (r[rows] * p_incl, kk[rows] * jnp.exp(cum - log_w[rows]), alpha[rows] * p_inv, k[rows] * p_inv,
                alpha[rows] * p_tail, k[rows] * p_tail, v[rows])
        ptot = jnp.exp(total)
        for pr in range(n_pairs):
            ln = slice(pr * LANES, (pr + 1) * LANES)
            slabs.append(tuple(z[:, ln] for z in full))
            ptots.append(ptot[:, ln])
    y_slabs = _rwkv_chunks(slabs, ptots, n_pairs, state_ref)
    y = cat([cat(y_slabs[c * n_pairs:(c + 1) * n_pairs], axis=1) for c in range(n_chunks)], axis=0)

    outs, sink_terms = [], []
    for i, s in enumerate(scores):
        blk, hd = divmod(i, n_att_heads)
        valid = band & ((kj >= ATT_BLOCK) | (t_blk > 0)) if blk == 0 else band
        s = jnp.where(valid, s, NEG_INF)
        sink = sinks_ref[hd]
        m = jnp.maximum(jnp.max(s, axis=-1, keepdims=True), sink)
        outs.append(_mm(jnp.exp(s - m), v_ops[i]))
        sink_terms.append(jnp.exp(sink - m))

    mean = head_sum(y) * (1.0 / HEAD)
    yc = y - mean

    att_blocks = []
    for blk in range(n_att_blocks):
        pair_slabs = []
        for pair in range(ATT_WIDTH // LANES):
            i0 = blk * n_att_heads + 2 * pair
            o0, o1 = outs[i0], outs[i0 + 1]
            num = jnp.where(out_lane0, o0, o1)
            den = pltpu.roll(jnp.where(out_lane0, o1, o0), HEAD, 1)
            den = den + jnp.where(out_lane0, sink_terms[i0], sink_terms[i0 + 1])
            pair_slabs.append(num / den)
        att_blocks.append(cat(pair_slabs, axis=1))
    y_att = cat(att_blocks, axis=0)

    br_att = _mm(y_att * (g_att * _sigmoid(g_att)), w_br_att_ref[...])
    var = head_sum(yc * yc) * (1.0 / HEAD)
    y = yc * lax.rsqrt(var + GN_EPS) * gn_w_ref[...] + gn_b_ref[...]
    y_rwkv = y + bonus * v
    br_rwkv = _mm(y_rwkv * (g_rwkv * _sigmoid(g_rwkv)), w_br_rwkv_ref[...])
    merged = _sigmoid(gate_rwkv) * br_rwkv + _sigmoid(gate_att) * br_att
    resid = x + _mm(merged, w_out_ref[...])
    out_ref[0] = _rms_norm(resid, g_final_ref[...])


def _const_spec(shape):
    return pl.BlockSpec(shape, lambda b, t: (0,) * len(shape), pipeline_mode=pl.Buffered(1))


def kernel(x, g_pre, w_in, mu_shift, w0, w_decay_up, a0, w_iclr_up, k_k, k_a, r_k, gn_w, gn_b, b_qkv,
           sinks, w_branch_rwkv, w_branch_att, w_out, g_final):
    assert g_pre.shape[0] == 1, "single layer"
    batch, seq, d = x.shape
    assert d == D_MODEL and seq % TOKENS_PER_STEP == 0 and w_in.shape[-1] == IN_COLS
    row = lambda p: p.reshape(1, -1).astype(F32)
    w_lora = jnp.zeros((2 * LORA, 2 * RWKV_WIDTH), F32)
    w_lora = w_lora.at[:LORA, :RWKV_WIDTH].set(w_decay_up[0]).at[LORA:, RWKV_WIDTH:].set(w_iclr_up[0])
    lane_head = jnp.arange(RWKV_WIDTH) // HEAD
    head_ones = (lane_head[:, None] == lane_head[None, :]).astype(BF16)
    tri = (jnp.arange(CHUNK)[:, None] >= jnp.arange(CHUNK)[None, :]).astype(BF16)
    operands = [
        x, row(g_pre[0]), w_in[0].astype(BF16), row(mu_shift[0]), row(w0[0]), w_lora.astype(BF16), row(a0[0]),
        row(k_k[0]), row(k_a[0]), row(r_k[0]), row(gn_w[0]), row(gn_b[0]), row(b_qkv[0]),
        w_branch_rwkv[0].astype(BF16), w_branch_att[0].astype(BF16), w_out[0].astype(BF16), row(g_final),
        head_ones, tri,
    ]
    x_spec = pl.BlockSpec((1, TOKENS_PER_STEP, D_MODEL), lambda b, t: (b, t, 0))
    in_specs = [pl.BlockSpec(memory_space=pltpu.SMEM), x_spec] + [_const_spec(op.shape) for op in operands[1:]]
    return pl.pallas_call(
        _layer_kernel,
        grid=(batch, seq // TOKENS_PER_STEP),
        in_specs=in_specs,
        out_specs=x_spec,
        out_shape=jax.ShapeDtypeStruct(x.shape, x.dtype),
        scratch_shapes=[
            pltpu.VMEM((RWKV_WIDTH // LANES, LANES, LANES), F32),
            pltpu.VMEM((8, SHIFT_WIDTH), F32),
            pltpu.VMEM((ATT_BLOCK, ATT_KV_WIDTH), F32),
            pltpu.VMEM((ATT_BLOCK, ATT_KV_WIDTH), F32),
        ],
        compiler_params=pltpu.CompilerParams(
            dimension_semantics=("arbitrary", "arbitrary"),
            vmem_limit_bytes=VMEM_LIMIT_BYTES,
        ),
        name="hybrid_rwkv7_swa_layer",
    )(sinks[0].astype(F32), *operands)
```

```python
import functools
import math

import jax
import jax.numpy as jnp
from jax import lax
from jax.experimental import pallas as pl
from jax.experimental.pallas import tpu as pltpu

D_MODEL = 1024
HEAD = 64
HEAD_SHIFT = 6
RWKV_WIDTH = 512
LORA = 64
ATT_WIDTH = 512
ATT_KV_WIDTH = 128
ATT_GROUP = 4
ATT_BLOCK = 128
QKV_WIDTH = ATT_WIDTH + 2 * ATT_KV_WIDTH
RKV_WIDTH = 3 * RWKV_WIDTH
SHIFT_WIDTH = RKV_WIDTH + 2 * LORA
REF_COLS = {"rkv": (0, RKV_WIDTH), "lo": (RKV_WIDTH, SHIFT_WIDTH)}
_off = SHIFT_WIDTH
for _name, _width in (("g_rwkv", RWKV_WIDTH), ("qkv", QKV_WIDTH), ("g_att", ATT_WIDTH),
                      ("gate_rwkv", D_MODEL), ("gate_att", D_MODEL)):
    REF_COLS[_name] = (_off, _off + _width)
    _off += _width
IN_COLS = _off
PROJ_ORDER = ("rkv", "qkv", "g_rwkv", "g_att", "gate_rwkv", "gate_att", "lo")
PROJ_COLS = {}
_off = 0
for _name in PROJ_ORDER:
    _width = REF_COLS[_name][1] - REF_COLS[_name][0]
    PROJ_COLS[_name] = (_off, _off + _width)
    _off += _width
RMS_EPS = 1e-6
GN_EPS = 64e-5
L2_EPS = 1e-12
NEG_INF = -1e30

LANES = 128
MXU_TILE = 256
SLAB = 128
TOKENS_PER_STEP = 256
CHUNK = 64
FILL_STEPS = 64
SQUARINGS = int(math.log2(CHUNK))
VMEM_LIMIT_BYTES = 56 * 1024 * 1024

F32 = jnp.float32
BF16 = jnp.bfloat16


def _mm(a, b, *, tb=False, passes=1):
    dn = (((1,), (1 if tb else 0,)), ((), ()))
    dot = lambda x, y: lax.dot_general(x, y, dn, preferred_element_type=F32)
    a_hi = a.astype(BF16)
    b_hi = b.astype(BF16)
    out = dot(a_hi, b_hi)
    if passes >= 2:
        a_lo = (a - a_hi.astype(F32)).astype(BF16)
        out = out + dot(a_lo, b_hi)
    if passes >= 3:
        b_lo = (b - b_hi.astype(F32)).astype(BF16)
        out = out + dot(a_hi, b_lo)
    return out


def _sigmoid(x):
    return 1.0 / (1.0 + jnp.exp(-x))


def _rms_norm(x, g):
    return x * lax.rsqrt(jnp.mean(x * x, axis=-1, keepdims=True) + RMS_EPS) * g


def _iota(shape, dim):
    return lax.broadcasted_iota(jnp.int32, shape, dim)


def _block_diag(z, mask):
    return jnp.where(mask, jnp.concatenate([z] * (SLAB // CHUNK), axis=0), 0.0)


class _Deferred:
    def __init__(self):
        self._queue = []

    def add(self, thunk, cost):
        self._queue.append((thunk, cost))

    def run(self, budget):
        while self._queue and budget > 0:
            thunk, cost = self._queue.pop(0)
            thunk()
            budget -= cost

    def drain(self):
        self.run(float("inf"))


def _rwkv_chunks(slabs, ptots, n_groups, state_ref, fill):
    heads = SLAB // HEAD
    row = _iota((CHUNK, SLAB), 0)
    col = _iota((CHUNK, SLAB), 1)
    src = col & (HEAD - 1)
    strict = src < row
    incl = src <= row
    eye = src == row
    lane_head = col >> HEAD_SHIFT
    r2 = _iota((SLAB, SLAB), 0)
    c2 = _iota((SLAB, SLAB), 1)
    bd_mask = (r2 >> HEAD_SHIFT) == (c2 >> HEAD_SHIFT)
    bd = functools.partial(_block_diag, mask=bd_mask)
    cat = jnp.concatenate
    n = len(slabs)

    a_all = []
    for rt, bt, at, kt, _, _, _ in slabs:
        rhs_t = cat([jnp.where(lane_head == hd, z, 0.0) for z in (at, kt) for hd in range(heads)], axis=0)
        a_all.append(_mm(cat([bt, rt], axis=0), rhs_t, tb=True))
    a_ba = [jnp.where(strict, a[:CHUNK, :SLAB], 0.0) for a in a_all]
    a_bk = [jnp.where(strict, a[:CHUNK, SLAB:], 0.0) for a in a_all]
    a_ra = [jnp.where(incl, a[CHUNK:, :SLAB], 0.0) for a in a_all]
    a_rk = [jnp.where(incl, a[CHUNK:, SLAB:], 0.0) for a in a_all]
    fill(FILL_STEPS)

    p = [_mm(a, bd(a)) for a in a_ba]
    av = [_mm(cat([a_bk[i], a_rk[i]], axis=0), bd(slabs[i][6])) for i in range(n)]
    fill(FILL_STEPS)

    x = [jnp.where(eye, 1.0, 0.0) + a for a in a_ba]
    for level in range(2, SQUARINGS + 1):
        if level < SQUARINGS:
            prod = [_mm(cat([x[i], p[i]], axis=0), bd(p[i])) for i in range(n)]
            x = [x[i] + prod[i][:CHUNK] for i in range(n)]
            p = [prod[i][CHUNK:] for i in range(n)]
        else:
            x = [x[i] + _mm(x[i], bd(p[i])) for i in range(n)]
        fill(FILL_STEPS)

    gw = [_mm(x[i], cat([bd(slabs[i][1]), bd(av[i][:CHUNK])], axis=1)) for i in range(n)]
    fill(FILL_STEPS)
    rg = [_mm(a_ra[i], cat([bd(gw[i][:, :SLAB]), bd(gw[i][:, SLAB:])], axis=1)) for i in range(n)]
    r_hat = [slabs[i][0] + rg[i][:, :SLAB] for i in range(n)]
    y0 = [rg[i][:, SLAB:] + av[i][CHUNK:] for i in range(n)]

    tiles = SLAB // LANES
    mn = []
    for i in range(n):
        _, _, _, _, ah, kh, v = slabs[i]
        for tl in range(tiles):
            ln = slice(tl * LANES, (tl + 1) * LANES)
            z_t = cat([ah[:, ln], kh[:, ln]], axis=0).T
            rhs = cat([cat([gw[i][:, ln], gw[i][:, SLAB + tl * LANES:SLAB + (tl + 1) * LANES]], axis=1),
                       cat([jnp.zeros_like(v[:, ln]), v[:, ln]], axis=1)], axis=0)
            mn.append(_mm(z_t, rhs))
    fill(FILL_STEPS)
    zero = jnp.zeros((LANES, LANES), F32)
    tile_diag = lambda blocks: cat([cat([blk if j == tl else zero for j in range(tiles)], axis=1)
                                    for tl, blk in enumerate(blocks)], axis=0)
    m_bd, n_bd = [], []
    for i in range(n):
        parts = mn[i * tiles:(i + 1) * tiles]
        m_full = tile_diag([q[:, :LANES] for q in parts])
        n_full = tile_diag([q[:, LANES:] for q in parts])
        m_bd.append(jnp.where(bd_mask, m_full, 0.0) + jnp.where(r2 == c2, ptots[i], 0.0))
        n_bd.append(jnp.where(bd_mask, n_full, 0.0))

    state = [state_ref[gr] for gr in range(n_groups)]
    ys = []
    for i in range(n):
        gr = i % n_groups
        out = _mm(cat([r_hat[i], m_bd[i]], axis=0), state[gr])
        ys.append(out[:CHUNK] + y0[i])
        state[gr] = out[CHUNK:] + n_bd[i]
        if gr == n_groups - 1:
            fill(FILL_STEPS)
    for gr in range(n_groups):
        state_ref[gr] = state[gr]
    return ys


def _layer_kernel(sinks_ref, x_ref, g_pre_ref, w_in_ref, mu_rkv_ref, mu_lo_ref, w0_ref, w_lora_ref, a0_ref,
                  k_k_ref, k_a_ref, r_k_ref, gn_w_ref, gn_b_ref, b_qkv_ref, w_br_rwkv_ref, w_br_att_ref,
                  w_out_ref, g_final_ref, head_ones_ref, tri_ref, out_ref,
                  state_ref, tail_rkv_ref, tail_lo_ref, k_prev_ref, v_prev_ref):
    tb = TOKENS_PER_STEP
    t_blk = pl.program_id(1)

    @pl.when(t_blk == 0)
    def _():
        state_ref[...] = jnp.zeros_like(state_ref)
        tail_rkv_ref[...] = jnp.zeros_like(tail_rkv_ref)
        tail_lo_ref[...] = jnp.zeros_like(tail_lo_ref)
        k_prev_ref[...] = jnp.zeros_like(k_prev_ref)
        v_prev_ref[...] = jnp.zeros_like(v_prev_ref)

    cat = jnp.concatenate
    x = x_ref[0]
    h = _rms_norm(x, g_pre_ref[...]).astype(BF16)

    def proj(name, tile=None):
        lo, hi = PROJ_COLS[name]
        if tile is not None:
            lo, hi = lo + tile * MXU_TILE, lo + (tile + 1) * MXU_TILE
        return jnp.dot(h, w_in_ref[:, lo:hi], preferred_element_type=F32)

    def token_shift(p, tail_ref, mu_ref):
        prev = pltpu.roll(p, 1, 0)
        prev = jnp.where(_iota(p.shape, 0) == 0, tail_ref[7:8, :], prev)
        tail_ref[...] = p[tb - 8:, :]
        return p + (prev - p) * mu_ref[...]

    p_lo = proj("lo")
    p_rkv = proj("rkv")
    qkv = proj("qkv") + b_qkv_ref[...]

    late = {}
    deferred = _Deferred()
    for name in ("g_rwkv", "g_att", "gate_rwkv", "gate_att"):
        for tile in range((PROJ_COLS[name][1] - PROJ_COLS[name][0]) // MXU_TILE):
            deferred.add(functools.partial(lambda nm, tl: late.__setitem__((nm, tl), proj(nm, tl)), name, tile),
                         tb // 16)
    late_proj = lambda nm: cat([late[nm, tl] for tl in range((PROJ_COLS[nm][1] - PROJ_COLS[nm][0]) // MXU_TILE)],
                               axis=1)

    lo = token_shift(p_lo, tail_lo_ref, mu_lo_ref)
    lo = jnp.where(_iota(lo.shape, 1) < LORA, jnp.tanh(lo), lo)
    lora = _mm(lo, w_lora_ref[...])

    q = qkv[:, :ATT_WIDTH] * (HEAD ** -0.5)
    k_att = qkv[:, ATT_WIDTH:ATT_WIDTH + ATT_KV_WIDTH]
    v_att = qkv[:, ATT_WIDTH + ATT_KV_WIDTH:]
    qi = _iota((ATT_BLOCK, 2 * ATT_BLOCK), 0)
    kj = _iota((ATT_BLOCK, 2 * ATT_BLOCK), 1)
    band = (kj > qi) & (kj <= qi + ATT_BLOCK)
    kv_lane0 = _iota((2 * ATT_BLOCK, LANES), 1) < HEAD
    out_lane0 = _iota((ATT_BLOCK, LANES), 1) < HEAD
    n_att_heads = ATT_WIDTH // HEAD
    n_att_blocks = tb // ATT_BLOCK
    k_prev = k_prev_ref[...]
    v_prev = v_prev_ref[...]
    scores, v_ops = [], []
    for blk in range(n_att_blocks):
        rows = slice(blk * ATT_BLOCK, (blk + 1) * ATT_BLOCK)
        k_cat = cat([k_prev, k_att[rows]], axis=0)
        v_cat = cat([v_prev, v_att[rows]], axis=0)
        k_swap = pltpu.roll(k_cat, HEAD, 1)
        v_swap = pltpu.roll(v_cat, HEAD, 1)
        k_ops, v_blk = {}, {}
        for kv_head in range(ATT_KV_WIDTH // HEAD):
            for parity in range(2):
                own_lanes = kv_lane0 if parity == 0 else ~kv_lane0
                k_ops[kv_head, parity] = jnp.where(own_lanes, k_cat if kv_head == parity else k_swap, 0.0)
                v_blk[kv_head, parity] = jnp.where(own_lanes, v_cat if kv_head == parity else v_swap, 1.0)
        for hd in range(n_att_heads):
            kv_head, parity, pair = hd // ATT_GROUP, hd % 2, hd // 2
            scores.append(_mm(q[rows, pair * LANES:(pair + 1) * LANES], k_ops[kv_head, parity], tb=True))
            v_ops.append(v_blk[kv_head, parity])
        k_prev, v_prev = k_att[rows], v_att[rows]
    k_prev_ref[...] = k_prev
    v_prev_ref[...] = v_prev

    outs, sink_terms = {}, {}

    def softmax_pv(i):
        blk, hd = divmod(i, n_att_heads)
        valid = band & ((kj >= ATT_BLOCK) | (t_blk > 0)) if blk == 0 else band
        s = jnp.where(valid, scores[i], NEG_INF)
        sink = sinks_ref[hd]
        m = jnp.maximum(jnp.max(s, axis=-1, keepdims=True), sink)
        outs[i] = _mm(jnp.exp(s - m), v_ops[i])
        sink_terms[i] = jnp.exp(sink - m)

    for i in range(len(scores)):
        deferred.add(functools.partial(softmax_pv, i), ATT_BLOCK // 16)

    rkv = token_shift(p_rkv, tail_rkv_ref, mu_rkv_ref)
    r = rkv[:, 0:RWKV_WIDTH]
    k = rkv[:, RWKV_WIDTH:2 * RWKV_WIDTH]
    v = rkv[:, 2 * RWKV_WIDTH:]
    head_ones = head_ones_ref[...]
    head_sum = lambda z: cat([_mm(z[:, j:j + MXU_TILE], head_ones) for j in range(0, RWKV_WIDTH, MXU_TILE)], axis=1)
    kk = k * k_k_ref[...]
    kk_norm2 = head_sum(kk * kk)
    log_w = -math.exp(-0.5) * _sigmoid(w0_ref[...] + lora[:, :RWKV_WIDTH])
    a = _sigmoid(a0_ref[...] + lora[:, RWKV_WIDTH:])

    n_groups = RWKV_WIDTH // SLAB
    n_chunks = tb // CHUNK
    tri = tri_ref[...]
    cums = []
    for c in range(n_chunks):
        lw = log_w[c * CHUNK:(c + 1) * CHUNK]
        lw_hi = lw.astype(BF16)
        lw_mid = (lw - lw_hi.astype(F32)).astype(BF16)
        lw_lo = (lw - lw_hi.astype(F32) - lw_mid.astype(F32)).astype(BF16)
        cums.append(jnp.dot(tri, lw_hi, preferred_element_type=F32)
                    + jnp.dot(tri, lw_mid, preferred_element_type=F32)
                    + jnp.dot(tri, lw_lo, preferred_element_type=F32))
    k = k * (1.0 + (a - 1.0) * k_a_ref[...])
    bonus = head_sum(r * k * r_k_ref[...])
    kk = kk / jnp.maximum(jnp.sqrt(kk_norm2), L2_EPS)
    alpha = -(a * kk)
    deferred.run(2 * FILL_STEPS)
    slabs, ptots = [], []
    for c in range(n_chunks):
        rows = slice(c * CHUNK, (c + 1) * CHUNK)
        cum = cums[c]
        total = cum[CHUNK - 1:CHUNK, :]
        p_incl = jnp.exp(cum)
        p_inv = jnp.exp(-cum)
        p_tail = jnp.exp(total - cum)
        full = (r[rows] * p_incl, kk[rows] * jnp.exp(cum - log_w[rows]), alpha[rows] * p_inv, k[rows] * p_inv,
                alpha[rows] * p_tail, k[rows] * p_tail, v[rows])
        ptot = jnp.exp(total)
        for gr in range(n_groups):
            ln = slice(gr * SLAB, (gr + 1) * SLAB)
            slabs.append(tuple(z[:, ln] for z in full))
            ptots.append(ptot[:, ln])
    y_slabs = _rwkv_chunks(slabs, ptots, n_groups, state_ref, deferred.run)
    y = cat([cat(y_slabs[c * n_groups:(c + 1) * n_groups], axis=1) for c in range(n_chunks)], axis=0)
    deferred.drain()

    mean = head_sum(y) * (1.0 / HEAD)
    yc = y - mean

    att_blocks = []
    for blk in range(n_att_blocks):
        pair_slabs = []
        for pair in range(ATT_WIDTH // LANES):
            i0 = blk * n_att_heads + 2 * pair
            o0, o1 = outs[i0], outs[i0 + 1]
            num = jnp.where(out_lane0, o0, o1)
            den = pltpu.roll(jnp.where(out_lane0, o1, o0), HEAD, 1)
            den = den + jnp.where(out_lane0, sink_terms[i0], sink_terms[i0 + 1])
            pair_slabs.append(num / den)
        att_blocks.append(cat(pair_slabs, axis=1))
    y_att = cat(att_blocks, axis=0)

    g_att = late_proj("g_att")
    br_att = _mm(y_att * (g_att * _sigmoid(g_att)), w_br_att_ref[...])
    var = head_sum(yc * yc) * (1.0 / HEAD)
    y = yc * lax.rsqrt(var + GN_EPS) * gn_w_ref[...] + gn_b_ref[...]
    y_rwkv = y + bonus * v
    g_rwkv = late_proj("g_rwkv")
    br_rwkv = _mm(y_rwkv * (g_rwkv * _sigmoid(g_rwkv)), w_br_rwkv_ref[...])
    merged = _sigmoid(late_proj("gate_rwkv")) * br_rwkv + _sigmoid(late_proj("gate_att")) * br_att
    resid = x + _mm(merged, w_out_ref[...])
    out_ref[0] = _rms_norm(resid, g_final_ref[...])


def _const_spec(shape):
    return pl.BlockSpec(shape, lambda b, t: (0,) * len(shape), pipeline_mode=pl.Buffered(1))


def kernel(x, g_pre, w_in, mu_shift, w0, w_decay_up, a0, w_iclr_up, k_k, k_a, r_k, gn_w, gn_b, b_qkv,
           sinks, w_branch_rwkv, w_branch_att, w_out, g_final):
    assert g_pre.shape[0] == 1, "single layer"
    batch, seq, d = x.shape
    assert d == D_MODEL and seq % TOKENS_PER_STEP == 0 and w_in.shape[-1] == IN_COLS
    row = lambda p: p.reshape(1, -1).astype(F32)
    w_lora = jnp.zeros((2 * LORA, 2 * RWKV_WIDTH), F32)
    w_lora = w_lora.at[:LORA, :RWKV_WIDTH].set(w_decay_up[0]).at[LORA:, RWKV_WIDTH:].set(w_iclr_up[0])
    lane_head = jnp.arange(MXU_TILE) // HEAD
    head_ones = (lane_head[:, None] == lane_head[None, :]).astype(BF16)
    tri = (jnp.arange(CHUNK)[:, None] >= jnp.arange(CHUNK)[None, :]).astype(BF16)
    w_in_k = jnp.concatenate([w_in[0][:, slice(*REF_COLS[name])] for name in PROJ_ORDER], axis=1)
    mu = mu_shift[0]
    operands = [
        x, row(g_pre[0]), w_in_k.astype(BF16), row(mu[slice(*REF_COLS["rkv"])]), row(mu[slice(*REF_COLS["lo"])]),
        row(w0[0]), w_lora.astype(BF16), row(a0[0]),
        row(k_k[0]), row(k_a[0]), row(r_k[0]), row(gn_w[0]), row(gn_b[0]), row(b_qkv[0]),
        w_branch_rwkv[0].astype(BF16), w_branch_att[0].astype(BF16), w_out[0].astype(BF16), row(g_final),
        head_ones, tri,
    ]
    x_spec = pl.BlockSpec((1, TOKENS_PER_STEP, D_MODEL), lambda b, t: (b, t, 0))
    in_specs = [pl.BlockSpec(memory_space=pltpu.SMEM), x_spec] + [_const_spec(op.shape) for op in operands[1:]]
    return pl.pallas_call(
        _layer_kernel,
        grid=(batch, seq // TOKENS_PER_STEP),
        in_specs=in_specs,
        out_specs=x_spec,
        out_shape=jax.ShapeDtypeStruct(x.shape, x.dtype),
        scratch_shapes=[
            pltpu.VMEM((RWKV_WIDTH // SLAB, SLAB, SLAB), F32),
            pltpu.VMEM((8, RKV_WIDTH), F32),
            pltpu.VMEM((8, 2 * LORA), F32),
            pltpu.VMEM((ATT_BLOCK, ATT_KV_WIDTH), F32),
            pltpu.VMEM((ATT_BLOCK, ATT_KV_WIDTH), F32),
        ],
        compiler_params=pltpu.CompilerParams(
            dimension_semantics=("arbitrary", "arbitrary"),
            vmem_limit_bytes=VMEM_LIMIT_BYTES,
        ),
        name="hybrid_rwkv7_swa_layer",
    )(sinks[0].astype(F32), *operands)
```

```python
import functools
import math

import jax
import jax.numpy as jnp
from jax import lax
from jax.experimental import pallas as pl
from jax.experimental.pallas import tpu as pltpu

D_MODEL = 1024
HEAD = 64
HEAD_SHIFT = 6
RWKV_WIDTH = 512
LORA = 64
ATT_WIDTH = 512
ATT_KV_WIDTH = 128
ATT_GROUP = 4
ATT_BLOCK = 128
QKV_WIDTH = ATT_WIDTH + 2 * ATT_KV_WIDTH
RKV_WIDTH = 3 * RWKV_WIDTH
SHIFT_WIDTH = RKV_WIDTH + 2 * LORA
PROJ_COLS = {"r": (0, RWKV_WIDTH), "k": (RWKV_WIDTH, 2 * RWKV_WIDTH), "v": (2 * RWKV_WIDTH, RKV_WIDTH),
             "lo": (RKV_WIDTH, SHIFT_WIDTH)}
_off = SHIFT_WIDTH
for _name, _width in (("g_rwkv", RWKV_WIDTH), ("qkv", QKV_WIDTH), ("g_att", ATT_WIDTH),
                      ("gate_rwkv", D_MODEL), ("gate_att", D_MODEL)):
    PROJ_COLS[_name] = (_off, _off + _width)
    _off += _width
IN_COLS = _off
RMS_EPS = 1e-6
GN_EPS = 64e-5
L2_EPS = 1e-12
NEG_INF = -1e30

LANES = 128
MXU_TILE = 256
SLAB = 128
TOKENS_PER_STEP = 256
CHUNK = 64
SQUARINGS = int(math.log2(CHUNK))
VMEM_LIMIT_BYTES = 56 * 1024 * 1024

F32 = jnp.float32
BF16 = jnp.bfloat16


def _mm(a, b, *, tb=False, passes=1):
    dn = (((1,), (1 if tb else 0,)), ((), ()))
    dot = lambda x, y: lax.dot_general(x, y, dn, preferred_element_type=F32)
    a_hi = a.astype(BF16)
    b_hi = b.astype(BF16)
    out = dot(a_hi, b_hi)
    if passes >= 2:
        a_lo = (a - a_hi.astype(F32)).astype(BF16)
        out = out + dot(a_lo, b_hi)
    if passes >= 3:
        b_lo = (b - b_hi.astype(F32)).astype(BF16)
        out = out + dot(a_hi, b_lo)
    return out


def _sigmoid(x):
    return 1.0 / (1.0 + jnp.exp(-x))


def _rms_norm(x, g):
    return x * lax.rsqrt(jnp.mean(x * x, axis=-1, keepdims=True) + RMS_EPS) * g


def _iota(shape, dim):
    return lax.broadcasted_iota(jnp.int32, shape, dim)


def _block_diag(z, mask):
    return jnp.where(mask, jnp.concatenate([z] * (SLAB // CHUNK), axis=0), 0.0)


class _Deferred:
    def __init__(self):
        self._queue = []

    def add(self, thunk, cost):
        self._queue.append((thunk, cost))

    def run(self, budget):
        while self._queue and budget > 0:
            thunk, cost = self._queue.pop(0)
            thunk()
            budget -= cost

    def drain(self):
        self.run(float("inf"))


def _rwkv_chunks(slabs, ptots, n_groups, state_ref, fill_wave, fill_chain):
    heads = SLAB // HEAD
    row = _iota((CHUNK, SLAB), 0)
    col = _iota((CHUNK, SLAB), 1)
    src = col & (HEAD - 1)
    strict = src < row
    incl = src <= row
    eye = src == row
    lane_head = col >> HEAD_SHIFT
    r2 = _iota((SLAB, SLAB), 0)
    c2 = _iota((SLAB, SLAB), 1)
    bd_mask = (r2 >> HEAD_SHIFT) == (c2 >> HEAD_SHIFT)
    bd = functools.partial(_block_diag, mask=bd_mask)
    cat = jnp.concatenate
    n = len(slabs)

    a_all = []
    for rt, bt, at, kt, _, _, _ in slabs:
        rhs_t = cat([jnp.where(lane_head == hd, z, 0.0) for z in (at, kt) for hd in range(heads)], axis=0)
        a_all.append(_mm(cat([bt, rt], axis=0), rhs_t, tb=True))
    a_ba = [jnp.where(strict, a[:CHUNK, :SLAB], 0.0) for a in a_all]
    a_bk = [jnp.where(strict, a[:CHUNK, SLAB:], 0.0) for a in a_all]
    a_ra = [jnp.where(incl, a[CHUNK:, :SLAB], 0.0) for a in a_all]
    a_rk = [jnp.where(incl, a[CHUNK:, SLAB:], 0.0) for a in a_all]
    fill_wave()

    p = [_mm(a, bd(a)) for a in a_ba]
    av = [_mm(cat([a_bk[i], a_rk[i]], axis=0), bd(slabs[i][6])) for i in range(n)]
    fill_wave()

    x = [jnp.where(eye, 1.0, 0.0) + a for a in a_ba]
    for level in range(2, SQUARINGS + 1):
        if level < SQUARINGS:
            prod = [_mm(cat([x[i], p[i]], axis=0), bd(p[i])) for i in range(n)]
            x = [x[i] + prod[i][:CHUNK] for i in range(n)]
            p = [prod[i][CHUNK:] for i in range(n)]
        else:
            x = [x[i] + _mm(x[i], bd(p[i])) for i in range(n)]
        fill_wave()

    gw = [_mm(x[i], cat([bd(slabs[i][1]), bd(av[i][:CHUNK])], axis=1)) for i in range(n)]
    fill_wave()
    rg = [_mm(a_ra[i], cat([bd(gw[i][:, :SLAB]), bd(gw[i][:, SLAB:])], axis=1)) for i in range(n)]
    r_hat = [slabs[i][0] + rg[i][:, :SLAB] for i in range(n)]
    y0 = [rg[i][:, SLAB:] + av[i][CHUNK:] for i in range(n)]

    tiles = SLAB // LANES
    mn = []
    for i in range(n):
        _, _, _, _, ah, kh, v = slabs[i]
        for tl in range(tiles):
            ln = slice(tl * LANES, (tl + 1) * LANES)
            z_t = cat([ah[:, ln], kh[:, ln]], axis=0).T
            rhs = cat([cat([gw[i][:, ln], gw[i][:, SLAB + tl * LANES:SLAB + (tl + 1) * LANES]], axis=1),
                       cat([jnp.zeros_like(v[:, ln]), v[:, ln]], axis=1)], axis=0)
            mn.append(_mm(z_t, rhs))
    fill_wave()
    zero = jnp.zeros((LANES, LANES), F32)
    tile_diag = lambda blocks: cat([cat([blk if j == tl else zero for j in range(tiles)], axis=1)
                                    for tl, blk in enumerate(blocks)], axis=0)
    m_bd, n_bd = [], []
    for i in range(n):
        parts = mn[i * tiles:(i + 1) * tiles]
        m_full = tile_diag([q[:, :LANES] for q in parts])
        n_full = tile_diag([q[:, LANES:] for q in parts])
        m_bd.append(jnp.where(bd_mask, m_full, 0.0) + jnp.where(r2 == c2, ptots[i], 0.0))
        n_bd.append(jnp.where(bd_mask, n_full, 0.0))

    state = [state_ref[gr] for gr in range(n_groups)]
    ys = []
    for i in range(n):
        gr = i % n_groups
        out = _mm(cat([r_hat[i], m_bd[i]], axis=0), state[gr])
        ys.append(out[:CHUNK] + y0[i])
        state[gr] = out[CHUNK:] + n_bd[i]
        if gr == n_groups - 1:
            fill_chain()
    for gr in range(n_groups):
        state_ref[gr] = state[gr]
    return ys


def _layer_kernel(sinks_ref, x_ref, g_pre_ref, w_in_ref, mu_ref, w0_ref, w_lora_ref, a0_ref,
                  k_k_ref, k_a_ref, r_k_ref, gn_w_ref, gn_b_ref, b_qkv_ref, w_br_rwkv_ref, w_br_att_ref,
                  w_out_ref, g_final_ref, head_ones_ref, tri_ref, out_ref,
                  state_ref, tail_ref, k_prev_ref, v_prev_ref):
    tb = TOKENS_PER_STEP
    t_blk = pl.program_id(1)

    @pl.when(t_blk == 0)
    def _():
        state_ref[...] = jnp.zeros_like(state_ref)
        tail_ref[...] = jnp.zeros_like(tail_ref)
        k_prev_ref[...] = jnp.zeros_like(k_prev_ref)
        v_prev_ref[...] = jnp.zeros_like(v_prev_ref)

    cat = jnp.concatenate
    x = x_ref[0]
    h = _rms_norm(x, g_pre_ref[...]).astype(BF16)

    tiles = {}

    def project_tile(j):
        tiles[j] = jnp.dot(h, w_in_ref[:, j * MXU_TILE:min((j + 1) * MXU_TILE, IN_COLS)], preferred_element_type=F32)

    def proj(name):
        lo, hi = PROJ_COLS[name]
        return cat([tiles[c // MXU_TILE][:, c % MXU_TILE:c % MXU_TILE + LANES] for c in range(lo, hi, LANES)], axis=1)

    def tiles_of(*names):
        return sorted({c // MXU_TILE for nm in names for c in range(*PROJ_COLS[nm], LANES)})

    def project(*names):
        for j in tiles_of(*names):
            if j not in tiles:
                project_tile(j)

    def token_shift(name):
        p = proj(name)
        cols = slice(PROJ_COLS[name][0], PROJ_COLS[name][1])
        prev = pltpu.roll(p, 1, 0)
        prev = jnp.where(_iota(p.shape, 0) == 0, tail_ref[7:8, cols], prev)
        tail_ref[:, cols] = p[tb - 8:, :]
        return p + (prev - p) * mu_ref[:, cols]

    head_ones = head_ones_ref[...]
    head_sum = lambda z: cat([_mm(z[:, j:j + MXU_TILE], head_ones) for j in range(0, RWKV_WIDTH, MXU_TILE)], axis=1)

    project("lo")
    project("r", "k")
    lo = token_shift("lo")
    lo = jnp.where(_iota(lo.shape, 1) < LORA, jnp.tanh(lo), lo)
    lora = _mm(lo, w_lora_ref[...])
    project("v")
    r = token_shift("r")
    k = token_shift("k")
    kk = k * k_k_ref[...]
    kk_norm2 = head_sum(kk * kk)
    project("qkv")
    log_w = -math.exp(-0.5) * _sigmoid(w0_ref[...] + lora[:, :RWKV_WIDTH])
    a = _sigmoid(a0_ref[...] + lora[:, RWKV_WIDTH:])
    n_groups = RWKV_WIDTH // SLAB
    n_chunks = tb // CHUNK
    tri = tri_ref[...]
    cums = []
    for c in range(n_chunks):
        lw = log_w[c * CHUNK:(c + 1) * CHUNK]
        lw_hi = lw.astype(BF16)
        lw_mid = (lw - lw_hi.astype(F32)).astype(BF16)
        lw_lo = (lw - lw_hi.astype(F32) - lw_mid.astype(F32)).astype(BF16)
        cums.append(jnp.dot(tri, lw_hi, preferred_element_type=F32)
                    + jnp.dot(tri, lw_mid, preferred_element_type=F32)
                    + jnp.dot(tri, lw_lo, preferred_element_type=F32))

    tile_steps = tb // 16
    deferred = _Deferred()
    for j in range(-(-IN_COLS // MXU_TILE)):
        if j not in tiles:
            deferred.add(functools.partial(project_tile, j), tile_steps)
    deferred.run(tile_steps)
    qkv = proj("qkv") + b_qkv_ref[...]

    q = qkv[:, :ATT_WIDTH] * (HEAD ** -0.5)
    k_att = qkv[:, ATT_WIDTH:ATT_WIDTH + ATT_KV_WIDTH]
    v_att = qkv[:, ATT_WIDTH + ATT_KV_WIDTH:]
    qi = _iota((ATT_BLOCK, 2 * ATT_BLOCK), 0)
    kj = _iota((ATT_BLOCK, 2 * ATT_BLOCK), 1)
    band = (kj > qi) & (kj <= qi + ATT_BLOCK)
    kv_lane0 = _iota((2 * ATT_BLOCK, LANES), 1) < HEAD
    out_lane0 = _iota((ATT_BLOCK, LANES), 1) < HEAD
    n_att_heads = ATT_WIDTH // HEAD
    n_att_blocks = tb // ATT_BLOCK
    k_prev = k_prev_ref[...]
    v_prev = v_prev_ref[...]
    scores, v_ops = [], []
    for blk in range(n_att_blocks):
        rows = slice(blk * ATT_BLOCK, (blk + 1) * ATT_BLOCK)
        k_cat = cat([k_prev, k_att[rows]], axis=0)
        v_cat = cat([v_prev, v_att[rows]], axis=0)
        k_swap = pltpu.roll(k_cat, HEAD, 1)
        v_swap = pltpu.roll(v_cat, HEAD, 1)
        k_ops, v_blk = {}, {}
        for kv_head in range(ATT_KV_WIDTH // HEAD):
            for parity in range(2):
                own_lanes = kv_lane0 if parity == 0 else ~kv_lane0
                k_ops[kv_head, parity] = jnp.where(own_lanes, k_cat if kv_head == parity else k_swap, 0.0)
                v_blk[kv_head, parity] = jnp.where(own_lanes, v_cat if kv_head == parity else v_swap, 1.0)
        for hd in range(n_att_heads):
            kv_head, parity, pair = hd // ATT_GROUP, hd % 2, hd // 2
            scores.append(_mm(q[rows, pair * LANES:(pair + 1) * LANES], k_ops[kv_head, parity], tb=True))
            v_ops.append(v_blk[kv_head, parity])
        k_prev, v_prev = k_att[rows], v_att[rows]
    k_prev_ref[...] = k_prev
    v_prev_ref[...] = v_prev

    deferred.run(tile_steps)

    small = _Deferred()
    outs, sink_terms = {}, {}

    def softmax_pv(i):
        blk, hd = divmod(i, n_att_heads)
        valid = band & ((kj >= ATT_BLOCK) | (t_blk > 0)) if blk == 0 else band
        s = jnp.where(valid, scores[i], NEG_INF)
        sink = sinks_ref[hd]
        m = jnp.maximum(jnp.max(s, axis=-1, keepdims=True), sink)
        outs[i] = _mm(jnp.exp(s - m), v_ops[i])
        sink_terms[i] = jnp.exp(sink - m)

    for i in range(len(scores)):
        small.add(functools.partial(softmax_pv, i), ATT_BLOCK // 16)

    v = token_shift("v")
    k = k * (1.0 + (a - 1.0) * k_a_ref[...])
    bonus = head_sum(r * k * r_k_ref[...])
    kk = kk * jnp.minimum(lax.rsqrt(kk_norm2), 1.0 / L2_EPS)
    alpha = -(a * kk)
    deferred.run(2 * tile_steps)
    slabs, ptots = [], []
    for c in range(n_chunks):
        rows = slice(c * CHUNK, (c + 1) * CHUNK)
        cum = cums[c]
        total = cum[CHUNK - 1:CHUNK, :]
        p_incl = jnp.exp(cum)
        p_inv = jnp.exp(-cum)
        p_tail = jnp.exp(total - cum)
        full = (r[rows] * p_incl, kk[rows] * jnp.exp(cum - log_w[rows]), alpha[rows] * p_inv, k[rows] * p_inv,
                alpha[rows] * p_tail, k[rows] * p_tail, v[rows])
        ptot = jnp.exp(total)
        for gr in range(n_groups):
            ln = slice(gr * SLAB, (gr + 1) * SLAB)
            slabs.append(tuple(z[:, ln] for z in full))
            ptots.append(ptot[:, ln])
    y_slabs = _rwkv_chunks(slabs, ptots, n_groups, state_ref,
                           functools.partial(small.run, 2 * ATT_BLOCK // 16), functools.partial(deferred.run, tile_steps))
    y = cat([cat(y_slabs[c * n_groups:(c + 1) * n_groups], axis=1) for c in range(n_chunks)], axis=0)
    small.drain()
    deferred.drain()

    mean = head_sum(y) * (1.0 / HEAD)
    yc = y - mean

    att_blocks = []
    for blk in range(n_att_blocks):
        pair_slabs = []
        for pair in range(ATT_WIDTH // LANES):
            i0 = blk * n_att_heads + 2 * pair
            o0, o1 = outs[i0], outs[i0 + 1]
            num = jnp.where(out_lane0, o0, o1)
            den = pltpu.roll(jnp.where(out_lane0, o1, o0), HEAD, 1)
            den = den + jnp.where(out_lane0, sink_terms[i0], sink_terms[i0 + 1])
            pair_slabs.append(num / den)
        att_blocks.append(cat(pair_slabs, axis=1))
    y_att = cat(att_blocks, axis=0)

    g_att = proj("g_att")
    br_att = _mm(y_att * (g_att * _sigmoid(g_att)), w_br_att_ref[...])
    var = head_sum(yc * yc) * (1.0 / HEAD)
    y = yc * lax.rsqrt(var + GN_EPS) * gn_w_ref[...] + gn_b_ref[...]
    y_rwkv = y + bonus * v
    g_rwkv = proj("g_rwkv")
    br_rwkv = _mm(y_rwkv * (g_rwkv * _sigmoid(g_rwkv)), w_br_rwkv_ref[...])
    merged = _sigmoid(proj("gate_rwkv")) * br_rwkv + _sigmoid(proj("gate_att")) * br_att
    resid = x + _mm(merged, w_out_ref[...])
    out_ref[0] = _rms_norm(resid, g_final_ref[...])


def _const_spec(shape):
    return pl.BlockSpec(shape, lambda b, t: (0,) * len(shape), pipeline_mode=pl.Buffered(1))


def kernel(x, g_pre, w_in, mu_shift, w0, w_decay_up, a0, w_iclr_up, k_k, k_a, r_k, gn_w, gn_b, b_qkv,
           sinks, w_branch_rwkv, w_branch_att, w_out, g_final):
    assert g_pre.shape[0] == 1, "single layer"
    batch, seq, d = x.shape
    assert d == D_MODEL and seq % TOKENS_PER_STEP == 0 and w_in.shape[-1] == IN_COLS
    row = lambda p: p.reshape(1, -1).astype(F32)
    w_lora = jnp.zeros((2 * LORA, 2 * RWKV_WIDTH), F32)
    w_lora = w_lora.at[:LORA, :RWKV_WIDTH].set(w_decay_up[0]).at[LORA:, RWKV_WIDTH:].set(w_iclr_up[0])
    lane_head = jnp.arange(MXU_TILE) // HEAD
    head_ones = (lane_head[:, None] == lane_head[None, :]).astype(BF16)
    tri = (jnp.arange(CHUNK)[:, None] >= jnp.arange(CHUNK)[None, :]).astype(BF16)
    operands = [
        x, row(g_pre[0]), w_in[0].astype(BF16), row(mu_shift[0]),
        row(w0[0]), w_lora.astype(BF16), row(a0[0]),
        row(k_k[0]), row(k_a[0]), row(r_k[0]), row(gn_w[0]), row(gn_b[0]), row(b_qkv[0]),
        w_branch_rwkv[0].astype(BF16), w_branch_att[0].astype(BF16), w_out[0].astype(BF16), row(g_final),
        head_ones, tri,
    ]
    x_spec = pl.BlockSpec((1, TOKENS_PER_STEP, D_MODEL), lambda b, t: (b, t, 0))
    in_specs = [pl.BlockSpec(memory_space=pltpu.SMEM), x_spec] + [_const_spec(op.shape) for op in operands[1:]]
    return pl.pallas_call(
        _layer_kernel,
        grid=(batch, seq // TOKENS_PER_STEP),
        in_specs=in_specs,
        out_specs=x_spec,
        out_shape=jax.ShapeDtypeStruct(x.shape, x.dtype),
        scratch_shapes=[
            pltpu.VMEM((RWKV_WIDTH // SLAB, SLAB, SLAB), F32),
            pltpu.VMEM((8, SHIFT_WIDTH), F32),
            pltpu.VMEM((ATT_BLOCK, ATT_KV_WIDTH), F32),
            pltpu.VMEM((ATT_BLOCK, ATT_KV_WIDTH), F32),
        ],
        compiler_params=pltpu.CompilerParams(
            dimension_semantics=("arbitrary", "arbitrary"),
            vmem_limit_bytes=VMEM_LIMIT_BYTES,
        ),
        name="hybrid_rwkv7_swa_layer",
    )(sinks[0].astype(F32), *operands)
```

```python
import functools
import math

import jax
import jax.numpy as jnp
from jax import lax
from jax.experimental import pallas as pl
from jax.experimental.pallas import tpu as pltpu

D_MODEL = 1024
HEAD = 64
HEAD_SHIFT = 6
RWKV_WIDTH = 512
LORA = 64
ATT_WIDTH = 512
ATT_KV_WIDTH = 128
ATT_GROUP = 4
ATT_BLOCK = 128
QKV_WIDTH = ATT_WIDTH + 2 * ATT_KV_WIDTH
RKV_WIDTH = 3 * RWKV_WIDTH
SHIFT_WIDTH = RKV_WIDTH + 2 * LORA
PROJ_COLS = {"r": (0, RWKV_WIDTH), "k": (RWKV_WIDTH, 2 * RWKV_WIDTH), "v": (2 * RWKV_WIDTH, RKV_WIDTH),
             "lo": (RKV_WIDTH, SHIFT_WIDTH)}
_off = SHIFT_WIDTH
for _name, _width in (("g_rwkv", RWKV_WIDTH), ("qkv", QKV_WIDTH), ("g_att", ATT_WIDTH),
                      ("gate_rwkv", D_MODEL), ("gate_att", D_MODEL)):
    PROJ_COLS[_name] = (_off, _off + _width)
    _off += _width
IN_COLS = _off
RMS_EPS = 1e-6
GN_EPS = 64e-5
L2_EPS = 1e-12
NEG_INF = -1e30

LANES = 128
MXU_TILE = 256
SLAB = 128
TOKENS_PER_STEP = 256
CHUNK = 64
SQUARINGS = int(math.log2(CHUNK))
VMEM_LIMIT_BYTES = 56 * 1024 * 1024

F32 = jnp.float32
BF16 = jnp.bfloat16


def _mm(a, b, *, tb=False, passes=1):
    dn = (((1,), (1 if tb else 0,)), ((), ()))
    dot = lambda x, y: lax.dot_general(x, y, dn, preferred_element_type=F32)
    a_hi = a.astype(BF16)
    b_hi = b.astype(BF16)
    out = dot(a_hi, b_hi)
    if passes >= 2:
        a_lo = (a - a_hi.astype(F32)).astype(BF16)
        out = out + dot(a_lo, b_hi)
    if passes >= 3:
        b_lo = (b - b_hi.astype(F32)).astype(BF16)
        out = out + dot(a_hi, b_lo)
    return out


def _sigmoid(x):
    return 1.0 / (1.0 + jnp.exp(-x))


def _rms_norm(x, g):
    return x * lax.rsqrt(jnp.mean(x * x, axis=-1, keepdims=True) + RMS_EPS) * g


def _iota(shape, dim):
    return lax.broadcasted_iota(jnp.int32, shape, dim)


def _block_diag(z, mask):
    return jnp.where(mask, jnp.concatenate([z] * (SLAB // CHUNK), axis=0), 0.0)


class _Deferred:
    def __init__(self):
        self._queue = []

    def add(self, thunk, cost):
        self._queue.append((thunk, cost))

    def run(self, budget):
        while self._queue and budget > 0:
            thunk, cost = self._queue.pop(0)
            thunk()
            budget -= cost

    def drain(self):
        self.run(float("inf"))


def _rwkv_chunks(slabs, ptots, n_groups, state_ref, fill_wave, fill_chain):
    heads = SLAB // HEAD
    row = _iota((CHUNK, SLAB), 0)
    col = _iota((CHUNK, SLAB), 1)
    src = col & (HEAD - 1)
    strict = src < row
    incl = src <= row
    eye = src == row
    lane_head = col >> HEAD_SHIFT
    r2 = _iota((SLAB, SLAB), 0)
    c2 = _iota((SLAB, SLAB), 1)
    bd_mask = (r2 >> HEAD_SHIFT) == (c2 >> HEAD_SHIFT)
    bd = functools.partial(_block_diag, mask=bd_mask)
    cat = jnp.concatenate
    n = len(slabs)

    a_all = []
    for rt, bt, at, kt, _, _, _ in slabs:
        rhs_t = cat([jnp.where(lane_head == hd, z, 0.0) for z in (at, kt) for hd in range(heads)], axis=0)
        a_all.append(_mm(cat([bt, rt], axis=0), rhs_t, tb=True))
    a_ba = [jnp.where(strict, a[:CHUNK, :SLAB], 0.0) for a in a_all]
    a_bk = [jnp.where(strict, a[:CHUNK, SLAB:], 0.0) for a in a_all]
    a_ra = [jnp.where(incl, a[CHUNK:, :SLAB], 0.0) for a in a_all]
    a_rk = [jnp.where(incl, a[CHUNK:, SLAB:], 0.0) for a in a_all]
    fill_wave()

    p = [_mm(a, bd(a)) for a in a_ba]
    av = [_mm(cat([a_bk[i], a_rk[i]], axis=0), bd(slabs[i][6])) for i in range(n)]
    fill_wave()

    x = [jnp.where(eye, 1.0, 0.0) + a for a in a_ba]
    for level in range(2, SQUARINGS + 1):
        if level < SQUARINGS:
            prod = [_mm(cat([x[i], p[i]], axis=0), bd(p[i])) for i in range(n)]
            x = [x[i] + prod[i][:CHUNK] for i in range(n)]
            p = [prod[i][CHUNK:] for i in range(n)]
        else:
            x = [x[i] + _mm(x[i], bd(p[i])) for i in range(n)]
        fill_wave()

    gw = [_mm(x[i], cat([bd(slabs[i][1]), bd(av[i][:CHUNK])], axis=1)) for i in range(n)]
    fill_wave()
    rg = [_mm(a_ra[i], cat([bd(gw[i][:, :SLAB]), bd(gw[i][:, SLAB:])], axis=1)) for i in range(n)]
    r_hat = [slabs[i][0] + rg[i][:, :SLAB] for i in range(n)]
    y0 = [rg[i][:, SLAB:] + av[i][CHUNK:] for i in range(n)]

    tiles = SLAB // LANES
    mn = []
    for i in range(n):
        _, _, _, _, ah, kh, v = slabs[i]
        for tl in range(tiles):
            ln = slice(tl * LANES, (tl + 1) * LANES)
            z_t = cat([ah[:, ln], kh[:, ln]], axis=0).T
            rhs = cat([cat([gw[i][:, ln], gw[i][:, SLAB + tl * LANES:SLAB + (tl + 1) * LANES]], axis=1),
                       cat([jnp.zeros_like(v[:, ln]), v[:, ln]], axis=1)], axis=0)
            mn.append(_mm(z_t, rhs))
    fill_wave()
    zero = jnp.zeros((LANES, LANES), F32)
    tile_diag = lambda blocks: cat([cat([blk if j == tl else zero for j in range(tiles)], axis=1)
                                    for tl, blk in enumerate(blocks)], axis=0)
    m_bd, n_bd = [], []
    for i in range(n):
        parts = mn[i * tiles:(i + 1) * tiles]
        m_full = tile_diag([q[:, :LANES] for q in parts])
        n_full = tile_diag([q[:, LANES:] for q in parts])
        m_bd.append(jnp.where(bd_mask, m_full, 0.0) + jnp.where(r2 == c2, ptots[i], 0.0))
        n_bd.append(jnp.where(bd_mask, n_full, 0.0))

    state = [state_ref[gr] for gr in range(n_groups)]
    ys = []
    for i in range(n):
        gr = i % n_groups
        out = _mm(cat([r_hat[i], m_bd[i]], axis=0), state[gr])
        ys.append(out[:CHUNK] + y0[i])
        state[gr] = out[CHUNK:] + n_bd[i]
        if gr == n_groups - 1:
            fill_chain()
    for gr in range(n_groups):
        state_ref[gr] = state[gr]
    return ys


def _layer_kernel(sinks_ref, x_ref, g_pre_ref, w_in_ref, mu_ref, w0_ref, w_lora_ref, a0_ref,
                  k_k_ref, k_a_ref, r_k_ref, gn_w_ref, gn_b_ref, b_qkv_ref, w_br_rwkv_ref, w_br_att_ref,
                  w_out_ref, g_final_ref, head_ones_ref, tri_ref, out_ref,
                  state_ref, tail_ref, k_prev_ref, v_prev_ref):
    tb = TOKENS_PER_STEP
    t_blk = pl.program_id(1)

    @pl.when(t_blk == 0)
    def _():
        state_ref[...] = jnp.zeros_like(state_ref)
        tail_ref[...] = jnp.zeros_like(tail_ref)
        k_prev_ref[...] = jnp.zeros_like(k_prev_ref)
        v_prev_ref[...] = jnp.zeros_like(v_prev_ref)

    cat = jnp.concatenate
    x = x_ref[0]
    h = _rms_norm(x, g_pre_ref[...]).astype(BF16)

    tiles = {}

    def project_tile(j):
        tiles[j] = jnp.dot(h, w_in_ref[:, j * MXU_TILE:min((j + 1) * MXU_TILE, IN_COLS)], preferred_element_type=F32)

    def proj(name):
        lo, hi = PROJ_COLS[name]
        return cat([tiles[c // MXU_TILE][:, c % MXU_TILE:c % MXU_TILE + LANES] for c in range(lo, hi, LANES)], axis=1)

    def tiles_of(*names):
        return sorted({c // MXU_TILE for nm in names for c in range(*PROJ_COLS[nm], LANES)})

    def project(*names):
        for j in tiles_of(*names):
            if j not in tiles:
                project_tile(j)

    def token_shift(name):
        p = proj(name)
        cols = slice(PROJ_COLS[name][0], PROJ_COLS[name][1])
        prev = pltpu.roll(p, 1, 0)
        prev = jnp.where(_iota(p.shape, 0) == 0, tail_ref[7:8, cols], prev)
        tail_ref[:, cols] = p[tb - 8:, :]
        return p + (prev - p) * mu_ref[:, cols]

    head_ones = head_ones_ref[...]
    head_sum = lambda z: cat([_mm(z[:, j:j + MXU_TILE], head_ones) for j in range(0, RWKV_WIDTH, MXU_TILE)], axis=1)

    project("lo")
    project("r", "k")
    lo = token_shift("lo")
    lo = jnp.where(_iota(lo.shape, 1) < LORA, jnp.tanh(lo), lo)
    lora = _mm(lo, w_lora_ref[...])
    project("v")
    r = token_shift("r")
    k = token_shift("k")
    kk = k * k_k_ref[...]
    kk_norm2 = head_sum(kk * kk)
    project("qkv")
    log_w = -math.exp(-0.5) * _sigmoid(w0_ref[...] + lora[:, :RWKV_WIDTH])
    a = _sigmoid(a0_ref[...] + lora[:, RWKV_WIDTH:])
    n_groups = RWKV_WIDTH // SLAB
    n_chunks = tb // CHUNK
    tri = tri_ref[...]
    cums = []
    for c in range(n_chunks):
        lw = log_w[c * CHUNK:(c + 1) * CHUNK]
        lw_hi = lw.astype(BF16)
        lw_lo = (lw - lw_hi.astype(F32)).astype(BF16)
        cums.append(jnp.dot(tri, lw_hi, preferred_element_type=F32) + jnp.dot(tri, lw_lo, preferred_element_type=F32))

    tile_steps = tb // 16
    deferred = _Deferred()
    for j in range(-(-IN_COLS // MXU_TILE)):
        if j not in tiles:
            deferred.add(functools.partial(project_tile, j), tile_steps)
    deferred.run(tile_steps)
    qkv = proj("qkv") + b_qkv_ref[...]

    q = qkv[:, :ATT_WIDTH] * (HEAD ** -0.5)
    k_att = qkv[:, ATT_WIDTH:ATT_WIDTH + ATT_KV_WIDTH]
    v_att = qkv[:, ATT_WIDTH + ATT_KV_WIDTH:]
    qi = _iota((ATT_BLOCK, 2 * ATT_BLOCK), 0)
    kj = _iota((ATT_BLOCK, 2 * ATT_BLOCK), 1)
    band = (kj > qi) & (kj <= qi + ATT_BLOCK)
    kv_lane0 = _iota((2 * ATT_BLOCK, LANES), 1) < HEAD
    out_lane0 = _iota((ATT_BLOCK, LANES), 1) < HEAD
    n_att_heads = ATT_WIDTH // HEAD
    n_att_blocks = tb // ATT_BLOCK
    k_prev = k_prev_ref[...]
    v_prev = v_prev_ref[...]
    scores, v_ops = [], []
    for blk in range(n_att_blocks):
        rows = slice(blk * ATT_BLOCK, (blk + 1) * ATT_BLOCK)
        k_cat = cat([k_prev, k_att[rows]], axis=0)
        v_cat = cat([v_prev, v_att[rows]], axis=0)
        k_swap = pltpu.roll(k_cat, HEAD, 1)
        v_swap = pltpu.roll(v_cat, HEAD, 1)
        k_ops, v_blk = {}, {}
        for kv_head in range(ATT_KV_WIDTH // HEAD):
            for parity in range(2):
                own_lanes = kv_lane0 if parity == 0 else ~kv_lane0
                k_ops[kv_head, parity] = jnp.where(own_lanes, k_cat if kv_head == parity else k_swap, 0.0)
                v_blk[kv_head, parity] = jnp.where(own_lanes, v_cat if kv_head == parity else v_swap, 1.0)
        for hd in range(n_att_heads):
            kv_head, parity, pair = hd // ATT_GROUP, hd % 2, hd // 2
            scores.append(_mm(q[rows, pair * LANES:(pair + 1) * LANES], k_ops[kv_head, parity], tb=True))
            v_ops.append(v_blk[kv_head, parity])
        k_prev, v_prev = k_att[rows], v_att[rows]
    k_prev_ref[...] = k_prev
    v_prev_ref[...] = v_prev

    deferred.run(tile_steps)

    small = _Deferred()
    outs, sink_terms = {}, {}

    def softmax_pv(i):
        blk, hd = divmod(i, n_att_heads)
        valid = band & ((kj >= ATT_BLOCK) | (t_blk > 0)) if blk == 0 else band
        s = jnp.where(valid, scores[i], NEG_INF)
        sink = sinks_ref[hd]
        m = jnp.maximum(jnp.max(s, axis=-1, keepdims=True), sink)
        outs[i] = _mm(jnp.exp(s - m), v_ops[i])
        sink_terms[i] = jnp.exp(sink - m)

    for i in range(len(scores)):
        small.add(functools.partial(softmax_pv, i), ATT_BLOCK // 16)

    v = token_shift("v")
    k = k * (1.0 + (a - 1.0) * k_a_ref[...])
    bonus = head_sum(r * k * r_k_ref[...])
    kk = kk * jnp.minimum(lax.rsqrt(kk_norm2), 1.0 / L2_EPS)
    alpha = -(a * kk)
    deferred.run(2 * tile_steps)
    slabs, ptots = [], []
    for c in range(n_chunks):
        rows = slice(c * CHUNK, (c + 1) * CHUNK)
        cum = cums[c]
        total = cum[CHUNK - 1:CHUNK, :]
        p_incl = jnp.exp(cum)
        p_inv = jnp.exp(-cum)
        p_tail = jnp.exp(total - cum)
        full = (r[rows] * p_incl, kk[rows] * jnp.exp(cum - log_w[rows]), alpha[rows] * p_inv, k[rows] * p_inv,
                alpha[rows] * p_tail, k[rows] * p_tail, v[rows])
        ptot = jnp.exp(total)
        for gr in range(n_groups):
            ln = slice(gr * SLAB, (gr + 1) * SLAB)
            slabs.append(tuple(z[:, ln] for z in full))
            ptots.append(ptot[:, ln])
    y_slabs = _rwkv_chunks(slabs, ptots, n_groups, state_ref,
                           functools.partial(small.run, 2 * ATT_BLOCK // 16), functools.partial(deferred.run, tile_steps))
    y = cat([cat(y_slabs[c * n_groups:(c + 1) * n_groups], axis=1) for c in range(n_chunks)], axis=0)
    small.drain()
    deferred.drain()

    mean = head_sum(y) * (1.0 / HEAD)
    yc = y - mean

    att_blocks = []
    for blk in range(n_att_blocks):
        pair_slabs = []
        for pair in range(ATT_WIDTH // LANES):
            i0 = blk * n_att_heads + 2 * pair
            o0, o1 = outs[i0], outs[i0 + 1]
            num = jnp.where(out_lane0, o0, o1)
            den = pltpu.roll(jnp.where(out_lane0, o1, o0), HEAD, 1)
            den = den + jnp.where(out_lane0, sink_terms[i0], sink_terms[i0 + 1])
            pair_slabs.append(num / den)
        att_blocks.append(cat(pair_slabs, axis=1))
    y_att = cat(att_blocks, axis=0)

    g_att = proj("g_att").astype(BF16)
    br_att = _mm(y_att.astype(BF16) * (g_att * _sigmoid(g_att)), w_br_att_ref[...])
    var = head_sum(yc * yc) * (1.0 / HEAD)
    y = yc * lax.rsqrt(var + GN_EPS) * gn_w_ref[...] + gn_b_ref[...]
    y_rwkv = y + bonus * v
    g_rwkv = proj("g_rwkv").astype(BF16)
    br_rwkv = _mm(y_rwkv.astype(BF16) * (g_rwkv * _sigmoid(g_rwkv)), w_br_rwkv_ref[...])
    merged = (_sigmoid(proj("gate_rwkv").astype(BF16)) * br_rwkv.astype(BF16)
              + _sigmoid(proj("gate_att").astype(BF16)) * br_att.astype(BF16))
    resid = x + _mm(merged, w_out_ref[...])
    out_ref[0] = _rms_norm(resid, g_final_ref[...])


def _const_spec(shape):
    return pl.BlockSpec(shape, lambda b, t: (0,) * len(shape), pipeline_mode=pl.Buffered(1))


def kernel(x, g_pre, w_in, mu_shift, w0, w_decay_up, a0, w_iclr_up, k_k, k_a, r_k, gn_w, gn_b, b_qkv,
           sinks, w_branch_rwkv, w_branch_att, w_out, g_final):
    assert g_pre.shape[0] == 1, "single layer"
    batch, seq, d = x.shape
    assert d == D_MODEL and seq % TOKENS_PER_STEP == 0 and w_in.shape[-1] == IN_COLS
    row = lambda p: p.reshape(1, -1).astype(F32)
    w_lora = jnp.zeros((2 * LORA, 2 * RWKV_WIDTH), F32)
    w_lora = w_lora.at[:LORA, :RWKV_WIDTH].set(w_decay_up[0]).at[LORA:, RWKV_WIDTH:].set(w_iclr_up[0])
    lane_head = jnp.arange(MXU_TILE) // HEAD
    head_ones = (lane_head[:, None] == lane_head[None, :]).astype(BF16)
    tri = (jnp.arange(CHUNK)[:, None] >= jnp.arange(CHUNK)[None, :]).astype(BF16)
    operands = [
        x, row(g_pre[0]), w_in[0].astype(BF16), row(mu_shift[0]),
        row(w0[0]), w_lora.astype(BF16), row(a0[0]),
        row(k_k[0]), row(k_a[0]), row(r_k[0]), row(gn_w[0]), row(gn_b[0]), row(b_qkv[0]),
        w_branch_rwkv[0].astype(BF16), w_branch_att[0].astype(BF16), w_out[0].astype(BF16), row(g_final),
        head_ones, tri,
    ]
    x_spec = pl.BlockSpec((1, TOKENS_PER_STEP, D_MODEL), lambda b, t: (b, t, 0))
    in_specs = [pl.BlockSpec(memory_space=pltpu.SMEM), x_spec] + [_const_spec(op.shape) for op in operands[1:]]
    return pl.pallas_call(
        _layer_kernel,
        grid=(batch, seq // TOKENS_PER_STEP),
        in_specs=in_specs,
        out_specs=x_spec,
        out_shape=jax.ShapeDtypeStruct(x.shape, x.dtype),
        scratch_shapes=[
            pltpu.VMEM((RWKV_WIDTH // SLAB, SLAB, SLAB), F32),
            pltpu.VMEM((8, SHIFT_WIDTH), F32),
            pltpu.VMEM((ATT_BLOCK, ATT_KV_WIDTH), F32),
            pltpu.VMEM((ATT_BLOCK, ATT_KV_WIDTH), F32),
        ],
        compiler_params=pltpu.CompilerParams(
            dimension_semantics=("arbitrary", "arbitrary"),
            vmem_limit_bytes=VMEM_LIMIT_BYTES,
        ),
        name="hybrid_rwkv7_swa_layer",
    )(sinks[0].astype(F32), *operands)
```

```python
import functools
import math

import jax
import jax.numpy as jnp
from jax import lax
from jax.experimental import pallas as pl
from jax.experimental.pallas import tpu as pltpu

D_MODEL = 1024
HEAD = 64
HEAD_SHIFT = 6
RWKV_WIDTH = 512
LORA = 64
ATT_WIDTH = 512
ATT_KV_WIDTH = 128
ATT_GROUP = 4
ATT_BLOCK = 128
QKV_WIDTH = ATT_WIDTH + 2 * ATT_KV_WIDTH
RKV_WIDTH = 3 * RWKV_WIDTH
SHIFT_WIDTH = RKV_WIDTH + 2 * LORA
PROJ_COLS = {"r": (0, RWKV_WIDTH), "k": (RWKV_WIDTH, 2 * RWKV_WIDTH), "v": (2 * RWKV_WIDTH, RKV_WIDTH),
             "lo": (RKV_WIDTH, SHIFT_WIDTH)}
_off = SHIFT_WIDTH
for _name, _width in (("g_rwkv", RWKV_WIDTH), ("qkv", QKV_WIDTH), ("g_att", ATT_WIDTH),
                      ("gate_rwkv", D_MODEL), ("gate_att", D_MODEL)):
    PROJ_COLS[_name] = (_off, _off + _width)
    _off += _width
IN_COLS = _off
RMS_EPS = 1e-6
GN_EPS = 64e-5
L2_EPS = 1e-12
NEG_INF = -1e30

LANES = 128
MXU_TILE = 256
SLAB = 128
TOKENS_PER_STEP = 256
CHUNK = 64
SQUARINGS = int(math.log2(CHUNK))
VMEM_LIMIT_BYTES = 56 * 1024 * 1024

F32 = jnp.float32
BF16 = jnp.bfloat16


def _mm(a, b, *, tb=False, passes=1):
    dn = (((1,), (1 if tb else 0,)), ((), ()))
    dot = lambda x, y: lax.dot_general(x, y, dn, preferred_element_type=F32)
    a_hi = a.astype(BF16)
    b_hi = b.astype(BF16)
    out = dot(a_hi, b_hi)
    if passes >= 2:
        a_lo = (a - a_hi.astype(F32)).astype(BF16)
        out = out + dot(a_lo, b_hi)
    if passes >= 3:
        b_lo = (b - b_hi.astype(F32)).astype(BF16)
        out = out + dot(a_hi, b_lo)
    return out


def _sigmoid(x):
    return 1.0 / (1.0 + jnp.exp(-x))


def _rms_norm(x, g):
    return x * lax.rsqrt(jnp.mean(x * x, axis=-1, keepdims=True) + RMS_EPS) * g


def _iota(shape, dim):
    return lax.broadcasted_iota(jnp.int32, shape, dim)


def _block_diag(z, mask):
    return jnp.where(mask, jnp.concatenate([z] * (SLAB // CHUNK), axis=0), 0.0)


class _Deferred:
    def __init__(self):
        self._queue = []

    def add(self, thunk, cost):
        self._queue.append((thunk, cost))

    def run(self, budget):
        while self._queue and budget > 0:
            thunk, cost = self._queue.pop(0)
            thunk()
            budget -= cost

    def drain(self):
        self.run(float("inf"))


def _rwkv_chunks(slabs, ptots, n_groups, state_ref, fill_wave, fill_chain):
    heads = SLAB // HEAD
    row = _iota((CHUNK, SLAB), 0)
    col = _iota((CHUNK, SLAB), 1)
    src = col & (HEAD - 1)
    strict = src < row
    incl = src <= row
    eye = src == row
    lane_head = col >> HEAD_SHIFT
    r2 = _iota((SLAB, SLAB), 0)
    c2 = _iota((SLAB, SLAB), 1)
    bd_mask = (r2 >> HEAD_SHIFT) == (c2 >> HEAD_SHIFT)
    bd = functools.partial(_block_diag, mask=bd_mask)
    cat = jnp.concatenate
    n = len(slabs)

    a_all = []
    for rt, bt, at, kt, _, _, _ in slabs:
        rhs_t = cat([jnp.where(lane_head == hd, z, 0.0) for z in (at, kt) for hd in range(heads)], axis=0)
        a_all.append(_mm(cat([bt, rt], axis=0), rhs_t, tb=True))
    a_ba = [jnp.where(strict, a[:CHUNK, :SLAB], 0.0) for a in a_all]
    a_bk = [jnp.where(strict, a[:CHUNK, SLAB:], 0.0) for a in a_all]
    a_ra = [jnp.where(incl, a[CHUNK:, :SLAB], 0.0) for a in a_all]
    a_rk = [jnp.where(incl, a[CHUNK:, SLAB:], 0.0) for a in a_all]
    fill_wave()

    p = [_mm(a, bd(a)) for a in a_ba]
    av = [_mm(cat([a_bk[i], a_rk[i]], axis=0), bd(slabs[i][6])) for i in range(n)]
    fill_wave()

    x = [jnp.where(eye, 1.0, 0.0) + a for a in a_ba]
    for level in range(2, SQUARINGS + 1):
        if level < SQUARINGS:
            prod = [_mm(cat([x[i], p[i]], axis=0), bd(p[i])) for i in range(n)]
            x = [x[i] + prod[i][:CHUNK] for i in range(n)]
            p = [prod[i][CHUNK:] for i in range(n)]
        else:
            x = [x[i] + _mm(x[i], bd(p[i])) for i in range(n)]
        fill_wave()

    gw = [_mm(x[i], cat([bd(slabs[i][1]), bd(av[i][:CHUNK])], axis=1)) for i in range(n)]
    fill_wave()
    rg = [_mm(a_ra[i], cat([bd(gw[i][:, :SLAB]), bd(gw[i][:, SLAB:])], axis=1)) for i in range(n)]
    r_hat = [slabs[i][0] + rg[i][:, :SLAB] for i in range(n)]
    y0 = [rg[i][:, SLAB:] + av[i][CHUNK:] for i in range(n)]

    tiles = SLAB // LANES
    mn = []
    for i in range(n):
        _, _, _, _, ah, kh, v = slabs[i]
        for tl in range(tiles):
            ln = slice(tl * LANES, (tl + 1) * LANES)
            z_t = cat([ah[:, ln], kh[:, ln]], axis=0).T
            rhs = cat([cat([gw[i][:, ln], gw[i][:, SLAB + tl * LANES:SLAB + (tl + 1) * LANES]], axis=1),
                       cat([jnp.zeros_like(v[:, ln]), v[:, ln]], axis=1)], axis=0)
            mn.append(_mm(z_t, rhs))
    fill_wave()
    zero = jnp.zeros((LANES, LANES), F32)
    tile_diag = lambda blocks: cat([cat([blk if j == tl else zero for j in range(tiles)], axis=1)
                                    for tl, blk in enumerate(blocks)], axis=0)
    m_bd, n_bd = [], []
    for i in range(n):
        parts = mn[i * tiles:(i + 1) * tiles]
        m_full = tile_diag([q[:, :LANES] for q in parts])
        n_full = tile_diag([q[:, LANES:] for q in parts])
        m_bd.append(jnp.where(bd_mask, m_full, 0.0) + jnp.where(r2 == c2, ptots[i], 0.0))
        n_bd.append(jnp.where(bd_mask, n_full, 0.0))

    state = [state_ref[gr] for gr in range(n_groups)]
    ys = []
    for i in range(n):
        gr = i % n_groups
        out = _mm(cat([r_hat[i], m_bd[i]], axis=0), state[gr])
        ys.append(out[:CHUNK] + y0[i])
        state[gr] = out[CHUNK:] + n_bd[i]
        if gr == n_groups - 1:
            fill_chain()
    for gr in range(n_groups):
        state_ref[gr] = state[gr]
    return ys


def _layer_kernel(sinks_ref, x_ref, g_pre_ref, w_in_ref, mu_ref, w0_ref, w_lora_ref, a0_ref,
                  k_k_ref, k_a_ref, r_k_ref, gn_w_ref, gn_b_ref, b_qkv_ref, w_br_rwkv_ref, w_br_att_ref,
                  w_out_ref, g_final_ref, head_ones_ref, tri_ref, out_ref,
                  merged_keep_ref, x_keep_ref, state_ref, tail_ref, k_prev_ref, v_prev_ref):
    tb = TOKENS_PER_STEP
    t_blk = pl.program_id(1)

    @pl.when(t_blk == 0)
    def _():
        merged_keep_ref[...] = jnp.zeros_like(merged_keep_ref)
        x_keep_ref[...] = jnp.zeros_like(x_keep_ref)
        state_ref[...] = jnp.zeros_like(state_ref)
        tail_ref[...] = jnp.zeros_like(tail_ref)
        k_prev_ref[...] = jnp.zeros_like(k_prev_ref)
        v_prev_ref[...] = jnp.zeros_like(v_prev_ref)

    cat = jnp.concatenate

    def finish(merged, resid_in, blk):
        rows = pl.ds(pl.multiple_of(lax.rem(blk + 2, 2) * tb, tb), tb)
        resid = resid_in + jnp.dot(merged, w_out_ref[...], preferred_element_type=F32)
        out_ref[0, rows, :] = _rms_norm(resid, g_final_ref[...])

    finish(merged_keep_ref[...], x_keep_ref[...], t_blk - 1)

    x = x_ref[0]
    h = _rms_norm(x, g_pre_ref[...]).astype(BF16)

    tiles = {}

    def project_tile(j):
        tiles[j] = jnp.dot(h, w_in_ref[:, j * MXU_TILE:min((j + 1) * MXU_TILE, IN_COLS)], preferred_element_type=F32)

    def proj(name):
        lo, hi = PROJ_COLS[name]
        return cat([tiles[c // MXU_TILE][:, c % MXU_TILE:c % MXU_TILE + LANES] for c in range(lo, hi, LANES)], axis=1)

    def tiles_of(*names):
        return sorted({c // MXU_TILE for nm in names for c in range(*PROJ_COLS[nm], LANES)})

    def project(*names):
        for j in tiles_of(*names):
            if j not in tiles:
                project_tile(j)

    def token_shift(name):
        p = proj(name)
        cols = slice(PROJ_COLS[name][0], PROJ_COLS[name][1])
        prev = pltpu.roll(p, 1, 0)
        prev = jnp.where(_iota(p.shape, 0) == 0, tail_ref[7:8, cols], prev)
        tail_ref[:, cols] = p[tb - 8:, :]
        return p + (prev - p) * mu_ref[:, cols]

    head_ones = head_ones_ref[...]
    head_sum = lambda z: cat([_mm(z[:, j:j + MXU_TILE], head_ones) for j in range(0, RWKV_WIDTH, MXU_TILE)], axis=1)

    project("lo")
    project("r", "k")
    lo = token_shift("lo")
    lo = jnp.where(_iota(lo.shape, 1) < LORA, jnp.tanh(lo), lo)
    lora = _mm(lo, w_lora_ref[...])
    project("v")
    r = token_shift("r")
    k = token_shift("k")
    kk = k * k_k_ref[...]
    kk_norm2 = head_sum(kk * kk)
    project("qkv")
    log_w = -math.exp(-0.5) * _sigmoid(w0_ref[...] + lora[:, :RWKV_WIDTH])
    a = _sigmoid(a0_ref[...] + lora[:, RWKV_WIDTH:])
    n_groups = RWKV_WIDTH // SLAB
    n_chunks = tb // CHUNK
    tri = tri_ref[...]
    cums = []
    for c in range(n_chunks):
        lw = log_w[c * CHUNK:(c + 1) * CHUNK]
        lw_hi = lw.astype(BF16)
        lw_mid = (lw - lw_hi.astype(F32)).astype(BF16)
        lw_lo = (lw - lw_hi.astype(F32) - lw_mid.astype(F32)).astype(BF16)
        cums.append(jnp.dot(tri, lw_hi, preferred_element_type=F32)
                    + jnp.dot(tri, lw_mid, preferred_element_type=F32)
                    + jnp.dot(tri, lw_lo, preferred_element_type=F32))

    tile_steps = tb // 16
    deferred = _Deferred()
    for j in range(-(-IN_COLS // MXU_TILE)):
        if j not in tiles:
            deferred.add(functools.partial(project_tile, j), tile_steps)
    deferred.run(tile_steps)
    qkv = proj("qkv") + b_qkv_ref[...]

    q = qkv[:, :ATT_WIDTH] * (HEAD ** -0.5)
    k_att = qkv[:, ATT_WIDTH:ATT_WIDTH + ATT_KV_WIDTH]
    v_att = qkv[:, ATT_WIDTH + ATT_KV_WIDTH:]
    qi = _iota((ATT_BLOCK, 2 * ATT_BLOCK), 0)
    kj = _iota((ATT_BLOCK, 2 * ATT_BLOCK), 1)
    band = (kj > qi) & (kj <= qi + ATT_BLOCK)
    kv_lane0 = _iota((2 * ATT_BLOCK, LANES), 1) < HEAD
    out_lane0 = _iota((ATT_BLOCK, LANES), 1) < HEAD
    n_att_heads = ATT_WIDTH // HEAD
    n_att_blocks = tb // ATT_BLOCK
    k_prev = k_prev_ref[...]
    v_prev = v_prev_ref[...]
    scores, v_ops = [], []
    for blk in range(n_att_blocks):
        rows = slice(blk * ATT_BLOCK, (blk + 1) * ATT_BLOCK)
        k_cat = cat([k_prev, k_att[rows]], axis=0)
        v_cat = cat([v_prev, v_att[rows]], axis=0)
        k_swap = pltpu.roll(k_cat, HEAD, 1)
        v_swap = pltpu.roll(v_cat, HEAD, 1)
        k_ops, v_blk = {}, {}
        for kv_head in range(ATT_KV_WIDTH // HEAD):
            for parity in range(2):
                own_lanes = kv_lane0 if parity == 0 else ~kv_lane0
                k_ops[kv_head, parity] = jnp.where(own_lanes, k_cat if kv_head == parity else k_swap, 0.0)
                v_blk[kv_head, parity] = jnp.where(own_lanes, v_cat if kv_head == parity else v_swap, 1.0)
        for hd in range(n_att_heads):
            kv_head, parity, pair = hd // ATT_GROUP, hd % 2, hd // 2
            scores.append(_mm(q[rows, pair * LANES:(pair + 1) * LANES], k_ops[kv_head, parity], tb=True))
            v_ops.append(v_blk[kv_head, parity])
        k_prev, v_prev = k_att[rows], v_att[rows]
    k_prev_ref[...] = k_prev
    v_prev_ref[...] = v_prev

    deferred.run(tile_steps)

    small = _Deferred()
    outs, sink_terms = {}, {}

    def softmax_pv(i):
        blk, hd = divmod(i, n_att_heads)
        valid = band & ((kj >= ATT_BLOCK) | (t_blk > 0)) if blk == 0 else band
        s = jnp.where(valid, scores[i], NEG_INF)
        sink = sinks_ref[hd]
        m = jnp.maximum(jnp.max(s, axis=-1, keepdims=True), sink)
        outs[i] = _mm(jnp.exp(s - m), v_ops[i])
        sink_terms[i] = jnp.exp(sink - m)

    for i in range(len(scores)):
        small.add(functools.partial(softmax_pv, i), ATT_BLOCK // 16)

    v = token_shift("v")
    k = k * (1.0 + (a - 1.0) * k_a_ref[...])
    bonus = head_sum(r * k * r_k_ref[...])
    kk = kk * jnp.minimum(lax.rsqrt(kk_norm2), 1.0 / L2_EPS)
    alpha = -(a * kk)
    deferred.run(2 * tile_steps)
    slabs, ptots = [], []
    for c in range(n_chunks):
        rows = slice(c * CHUNK, (c + 1) * CHUNK)
        cum = cums[c]
        total = cum[CHUNK - 1:CHUNK, :]
        p_incl = jnp.exp(cum)
        p_inv = jnp.exp(-cum)
        p_tail = jnp.exp(total - cum)
        full = (r[rows] * p_incl, kk[rows] * jnp.exp(cum - log_w[rows]), alpha[rows] * p_inv, k[rows] * p_inv,
                alpha[rows] * p_tail, k[rows] * p_tail, v[rows])
        ptot = jnp.exp(total)
        for gr in range(n_groups):
            ln = slice(gr * SLAB, (gr + 1) * SLAB)
            slabs.append(tuple(z[:, ln] for z in full))
            ptots.append(ptot[:, ln])
    y_slabs = _rwkv_chunks(slabs, ptots, n_groups, state_ref,
                           functools.partial(small.run, 2 * ATT_BLOCK // 16), functools.partial(deferred.run, tile_steps))
    y = cat([cat(y_slabs[c * n_groups:(c + 1) * n_groups], axis=1) for c in range(n_chunks)], axis=0)
    small.drain()
    deferred.drain()

    mean = head_sum(y) * (1.0 / HEAD)
    yc = y - mean

    att_blocks = []
    for blk in range(n_att_blocks):
        pair_slabs = []
        for pair in range(ATT_WIDTH // LANES):
            i0 = blk * n_att_heads + 2 * pair
            o0, o1 = outs[i0], outs[i0 + 1]
            num = jnp.where(out_lane0, o0, o1)
            den = pltpu.roll(jnp.where(out_lane0, o1, o0), HEAD, 1)
            den = den + jnp.where(out_lane0, sink_terms[i0], sink_terms[i0 + 1])
            pair_slabs.append(num / den)
        att_blocks.append(cat(pair_slabs, axis=1))
    y_att = cat(att_blocks, axis=0)

    g_att = proj("g_att")
    br_att = _mm(y_att * (g_att * _sigmoid(g_att)), w_br_att_ref[...])
    var = head_sum(yc * yc) * (1.0 / HEAD)
    y = yc * lax.rsqrt(var + GN_EPS) * gn_w_ref[...] + gn_b_ref[...]
    y_rwkv = y + bonus * v
    g_rwkv = proj("g_rwkv")
    br_rwkv = _mm(y_rwkv * (g_rwkv * _sigmoid(g_rwkv)), w_br_rwkv_ref[...])
    merged = (_sigmoid(proj("gate_rwkv")) * br_rwkv + _sigmoid(proj("gate_att")) * br_att).astype(BF16)
    merged_keep_ref[...] = merged
    x_keep_ref[...] = x

    @pl.when(t_blk == pl.num_programs(1) - 1)
    def _():
        finish(merged, x, t_blk)


def _const_spec(shape):
    return pl.BlockSpec(shape, lambda b, t: (0,) * len(shape), pipeline_mode=pl.Buffered(1))


def kernel(x, g_pre, w_in, mu_shift, w0, w_decay_up, a0, w_iclr_up, k_k, k_a, r_k, gn_w, gn_b, b_qkv,
           sinks, w_branch_rwkv, w_branch_att, w_out, g_final):
    assert g_pre.shape[0] == 1, "single layer"
    batch, seq, d = x.shape
    assert d == D_MODEL and seq % TOKENS_PER_STEP == 0 and w_in.shape[-1] == IN_COLS
    row = lambda p: p.reshape(1, -1).astype(F32)
    w_lora = jnp.zeros((2 * LORA, 2 * RWKV_WIDTH), F32)
    w_lora = w_lora.at[:LORA, :RWKV_WIDTH].set(w_decay_up[0]).at[LORA:, RWKV_WIDTH:].set(w_iclr_up[0])
    lane_head = jnp.arange(MXU_TILE) // HEAD
    head_ones = (lane_head[:, None] == lane_head[None, :]).astype(BF16)
    tri = (jnp.arange(CHUNK)[:, None] >= jnp.arange(CHUNK)[None, :]).astype(BF16)
    operands = [
        x, row(g_pre[0]), w_in[0].astype(BF16), row(mu_shift[0]),
        row(w0[0]), w_lora.astype(BF16), row(a0[0]),
        row(k_k[0]), row(k_a[0]), row(r_k[0]), row(gn_w[0]), row(gn_b[0]), row(b_qkv[0]),
        w_branch_rwkv[0].astype(BF16), w_branch_att[0].astype(BF16), w_out[0].astype(BF16), row(g_final),
        head_ones, tri,
    ]
    n_blocks = seq // TOKENS_PER_STEP
    assert n_blocks % 2 == 0
    x_spec = pl.BlockSpec((1, TOKENS_PER_STEP, D_MODEL), lambda b, t: (b, t, 0))
    out_spec = pl.BlockSpec((1, 2 * TOKENS_PER_STEP, D_MODEL), lambda b, t: (b, jnp.maximum(t - 1, 0) // 2, 0))
    in_specs = [pl.BlockSpec(memory_space=pltpu.SMEM), x_spec] + [_const_spec(op.shape) for op in operands[1:]]
    return pl.pallas_call(
        _layer_kernel,
        grid=(batch, n_blocks),
        in_specs=in_specs,
        out_specs=out_spec,
        out_shape=jax.ShapeDtypeStruct(x.shape, x.dtype),
        scratch_shapes=[
            pltpu.VMEM((TOKENS_PER_STEP, D_MODEL), BF16),
            pltpu.VMEM((TOKENS_PER_STEP, D_MODEL), F32),
            pltpu.VMEM((RWKV_WIDTH // SLAB, SLAB, SLAB), F32),
            pltpu.VMEM((8, SHIFT_WIDTH), F32),
            pltpu.VMEM((ATT_BLOCK, ATT_KV_WIDTH), F32),
            pltpu.VMEM((ATT_BLOCK, ATT_KV_WIDTH), F32),
        ],
        compiler_params=pltpu.CompilerParams(
            dimension_semantics=("arbitrary", "arbitrary"),
            vmem_limit_bytes=VMEM_LIMIT_BYTES,
        ),
        name="hybrid_rwkv7_swa_layer",
    )(sinks[0].astype(F32), *operands)
```

```python
import functools
import math

import jax
import jax.numpy as jnp
from jax import lax
from jax.experimental import pallas as pl
from jax.experimental.pallas import tpu as pltpu

D_MODEL = 1024
HEAD = 64
HEAD_SHIFT = 6
RWKV_WIDTH = 512
LORA = 64
ATT_WIDTH = 512
ATT_KV_WIDTH = 128
ATT_GROUP = 4
ATT_BLOCK = 128
QKV_WIDTH = ATT_WIDTH + 2 * ATT_KV_WIDTH
RKV_WIDTH = 3 * RWKV_WIDTH
SHIFT_WIDTH = RKV_WIDTH + 2 * LORA
PROJ_COLS = {"r": (0, RWKV_WIDTH), "k": (RWKV_WIDTH, 2 * RWKV_WIDTH), "v": (2 * RWKV_WIDTH, RKV_WIDTH),
             "lo": (RKV_WIDTH, SHIFT_WIDTH)}
_off = SHIFT_WIDTH
for _name, _width in (("g_rwkv", RWKV_WIDTH), ("qkv", QKV_WIDTH), ("g_att", ATT_WIDTH),
                      ("gate_rwkv", D_MODEL), ("gate_att", D_MODEL)):
    PROJ_COLS[_name] = (_off, _off + _width)
    _off += _width
IN_COLS = _off
RMS_EPS = 1e-6
GN_EPS = 64e-5
L2_EPS = 1e-12
NEG_INF = -1e30

LANES = 128
MXU_TILE = 256
SLAB = 128
TOKENS_PER_STEP = 512
CHUNK = 64
SQUARINGS = int(math.log2(CHUNK))
VMEM_LIMIT_BYTES = 56 * 1024 * 1024

F32 = jnp.float32
BF16 = jnp.bfloat16


def _mm(a, b, *, tb=False, passes=1):
    dn = (((1,), (1 if tb else 0,)), ((), ()))
    dot = lambda x, y: lax.dot_general(x, y, dn, preferred_element_type=F32)
    a_hi = a.astype(BF16)
    b_hi = b.astype(BF16)
    out = dot(a_hi, b_hi)
    if passes >= 2:
        a_lo = (a - a_hi.astype(F32)).astype(BF16)
        out = out + dot(a_lo, b_hi)
    if passes >= 3:
        b_lo = (b - b_hi.astype(F32)).astype(BF16)
        out = out + dot(a_hi, b_lo)
    return out


def _sigmoid(x):
    return 1.0 / (1.0 + jnp.exp(-x))


def _rms_norm(x, g):
    return x * lax.rsqrt(jnp.mean(x * x, axis=-1, keepdims=True) + RMS_EPS) * g


def _iota(shape, dim):
    return lax.broadcasted_iota(jnp.int32, shape, dim)


def _block_diag(z, mask):
    return jnp.where(mask, jnp.concatenate([z] * (SLAB // CHUNK), axis=0), 0.0)


class _Deferred:
    def __init__(self):
        self._queue = []

    def add(self, thunk, cost):
        self._queue.append((thunk, cost))

    def run(self, budget):
        while self._queue and budget > 0:
            thunk, cost = self._queue.pop(0)
            thunk()
            budget -= cost

    def drain(self):
        self.run(float("inf"))


def _rwkv_chunks(slabs, ptots, n_groups, state_ref, fill_wave, fill_chain):
    heads = SLAB // HEAD
    row = _iota((CHUNK, SLAB), 0)
    col = _iota((CHUNK, SLAB), 1)
    src = col & (HEAD - 1)
    strict = src < row
    incl = src <= row
    eye = src == row
    lane_head = col >> HEAD_SHIFT
    r2 = _iota((SLAB, SLAB), 0)
    c2 = _iota((SLAB, SLAB), 1)
    bd_mask = (r2 >> HEAD_SHIFT) == (c2 >> HEAD_SHIFT)
    bd = functools.partial(_block_diag, mask=bd_mask)
    cat = jnp.concatenate
    n = len(slabs)

    a_all = []
    for rt, bt, at, kt, _, _, _ in slabs:
        rhs_t = cat([jnp.where(lane_head == hd, z, 0.0) for z in (at, kt) for hd in range(heads)], axis=0)
        a_all.append(_mm(cat([bt, rt], axis=0), rhs_t, tb=True))
    a_ba = [jnp.where(strict, a[:CHUNK, :SLAB], 0.0) for a in a_all]
    a_bk = [jnp.where(strict, a[:CHUNK, SLAB:], 0.0) for a in a_all]
    a_ra = [jnp.where(incl, a[CHUNK:, :SLAB], 0.0) for a in a_all]
    a_rk = [jnp.where(incl, a[CHUNK:, SLAB:], 0.0) for a in a_all]
    fill_wave()

    p = [_mm(a, bd(a)) for a in a_ba]
    av = [_mm(cat([a_bk[i], a_rk[i]], axis=0), bd(slabs[i][6])) for i in range(n)]
    fill_wave()

    x = [jnp.where(eye, 1.0, 0.0) + a for a in a_ba]
    for level in range(2, SQUARINGS + 1):
        if level < SQUARINGS:
            prod = [_mm(cat([x[i], p[i]], axis=0), bd(p[i])) for i in range(n)]
            x = [x[i] + prod[i][:CHUNK] for i in range(n)]
            p = [prod[i][CHUNK:] for i in range(n)]
        else:
            x = [x[i] + _mm(x[i], bd(p[i])) for i in range(n)]
        fill_wave()

    gw = [_mm(x[i], cat([bd(slabs[i][1]), bd(av[i][:CHUNK])], axis=1)) for i in range(n)]
    fill_wave()
    rg = [_mm(a_ra[i], cat([bd(gw[i][:, :SLAB]), bd(gw[i][:, SLAB:])], axis=1)) for i in range(n)]
    r_hat = [slabs[i][0] + rg[i][:, :SLAB] for i in range(n)]
    y0 = [rg[i][:, SLAB:] + av[i][CHUNK:] for i in range(n)]

    tiles = SLAB // LANES
    mn = []
    for i in range(n):
        _, _, _, _, ah, kh, v = slabs[i]
        for tl in range(tiles):
            ln = slice(tl * LANES, (tl + 1) * LANES)
            z_t = cat([ah[:, ln], kh[:, ln]], axis=0).T
            rhs = cat([cat([gw[i][:, ln], gw[i][:, SLAB + tl * LANES:SLAB + (tl + 1) * LANES]], axis=1),
                       cat([jnp.zeros_like(v[:, ln]), v[:, ln]], axis=1)], axis=0)
            mn.append(_mm(z_t, rhs))
    fill_wave()
    zero = jnp.zeros((LANES, LANES), F32)
    tile_diag = lambda blocks: cat([cat([blk if j == tl else zero for j in range(tiles)], axis=1)
                                    for tl, blk in enumerate(blocks)], axis=0)
    m_bd, n_bd = [], []
    for i in range(n):
        parts = mn[i * tiles:(i + 1) * tiles]
        m_full = tile_diag([q[:, :LANES] for q in parts])
        n_full = tile_diag([q[:, LANES:] for q in parts])
        m_bd.append(jnp.where(bd_mask, m_full, 0.0) + jnp.where(r2 == c2, ptots[i], 0.0))
        n_bd.append(jnp.where(bd_mask, n_full, 0.0))

    state = [state_ref[gr] for gr in range(n_groups)]
    ys = []
    for i in range(n):
        gr = i % n_groups
        out = _mm(cat([r_hat[i], m_bd[i]], axis=0), state[gr])
        ys.append(out[:CHUNK] + y0[i])
        state[gr] = out[CHUNK:] + n_bd[i]
        if gr == n_groups - 1:
            fill_chain()
    for gr in range(n_groups):
        state_ref[gr] = state[gr]
    return ys


def _layer_kernel(sinks_ref, x_ref, g_pre_ref, w_in_ref, mu_ref, w0_ref, w_lora_ref, a0_ref,
                  k_k_ref, k_a_ref, r_k_ref, gn_w_ref, gn_b_ref, b_qkv_ref, w_br_rwkv_ref, w_br_att_ref,
                  w_out_ref, g_final_ref, head_ones_ref, tri_ref, out_ref,
                  state_ref, tail_ref, k_prev_ref, v_prev_ref):
    tb = TOKENS_PER_STEP
    t_blk = pl.program_id(1)

    @pl.when(t_blk == 0)
    def _():
        state_ref[...] = jnp.zeros_like(state_ref)
        tail_ref[...] = jnp.zeros_like(tail_ref)
        k_prev_ref[...] = jnp.zeros_like(k_prev_ref)
        v_prev_ref[...] = jnp.zeros_like(v_prev_ref)

    cat = jnp.concatenate
    x = x_ref[0]
    h = _rms_norm(x, g_pre_ref[...]).astype(BF16)

    tiles = {}

    def project_tile(j):
        tiles[j] = jnp.dot(h, w_in_ref[:, j * MXU_TILE:min((j + 1) * MXU_TILE, IN_COLS)], preferred_element_type=F32)

    def proj(name):
        lo, hi = PROJ_COLS[name]
        return cat([tiles[c // MXU_TILE][:, c % MXU_TILE:c % MXU_TILE + LANES] for c in range(lo, hi, LANES)], axis=1)

    def tiles_of(*names):
        return sorted({c // MXU_TILE for nm in names for c in range(*PROJ_COLS[nm], LANES)})

    def project(*names):
        for j in tiles_of(*names):
            if j not in tiles:
                project_tile(j)

    def token_shift(name):
        p = proj(name)
        cols = slice(PROJ_COLS[name][0], PROJ_COLS[name][1])
        prev = pltpu.roll(p, 1, 0)
        prev = jnp.where(_iota(p.shape, 0) == 0, tail_ref[7:8, cols], prev)
        tail_ref[:, cols] = p[tb - 8:, :]
        return p + (prev - p) * mu_ref[:, cols]

    head_ones = head_ones_ref[...]
    head_sum = lambda z: cat([_mm(z[:, j:j + MXU_TILE], head_ones) for j in range(0, RWKV_WIDTH, MXU_TILE)], axis=1)

    project("lo")
    project("r", "k")
    lo = token_shift("lo")
    lo = jnp.where(_iota(lo.shape, 1) < LORA, jnp.tanh(lo), lo)
    lora = _mm(lo, w_lora_ref[...])
    project("v")
    r = token_shift("r")
    k = token_shift("k")
    kk = k * k_k_ref[...]
    kk_norm2 = head_sum(kk * kk)
    project("qkv")
    log_w = -math.exp(-0.5) * _sigmoid(w0_ref[...] + lora[:, :RWKV_WIDTH])
    a = _sigmoid(a0_ref[...] + lora[:, RWKV_WIDTH:])
    n_groups = RWKV_WIDTH // SLAB
    n_chunks = tb // CHUNK
    tri = tri_ref[...]
    cums = []
    for c in range(n_chunks):
        lw = log_w[c * CHUNK:(c + 1) * CHUNK]
        lw_hi = lw.astype(BF16)
        lw_mid = (lw - lw_hi.astype(F32)).astype(BF16)
        lw_lo = (lw - lw_hi.astype(F32) - lw_mid.astype(F32)).astype(BF16)
        cums.append(jnp.dot(tri, lw_hi, preferred_element_type=F32)
                    + jnp.dot(tri, lw_mid, preferred_element_type=F32)
                    + jnp.dot(tri, lw_lo, preferred_element_type=F32))

    tile_steps = tb // 16
    deferred = _Deferred()
    for j in range(-(-IN_COLS // MXU_TILE)):
        if j not in tiles:
            deferred.add(functools.partial(project_tile, j), tile_steps)
    deferred.run(tile_steps)
    qkv = proj("qkv") + b_qkv_ref[...]

    q = qkv[:, :ATT_WIDTH] * (HEAD ** -0.5)
    k_att = qkv[:, ATT_WIDTH:ATT_WIDTH + ATT_KV_WIDTH]
    v_att = qkv[:, ATT_WIDTH + ATT_KV_WIDTH:]
    qi = _iota((ATT_BLOCK, 2 * ATT_BLOCK), 0)
    kj = _iota((ATT_BLOCK, 2 * ATT_BLOCK), 1)
    band = (kj > qi) & (kj <= qi + ATT_BLOCK)
    kv_lane0 = _iota((2 * ATT_BLOCK, LANES), 1) < HEAD
    out_lane0 = _iota((ATT_BLOCK, LANES), 1) < HEAD
    n_att_heads = ATT_WIDTH // HEAD
    n_att_blocks = tb // ATT_BLOCK
    k_prev = k_prev_ref[...]
    v_prev = v_prev_ref[...]
    scores, v_ops = [], []
    for blk in range(n_att_blocks):
        rows = slice(blk * ATT_BLOCK, (blk + 1) * ATT_BLOCK)
        k_cat = cat([k_prev, k_att[rows]], axis=0)
        v_cat = cat([v_prev, v_att[rows]], axis=0)
        k_swap = pltpu.roll(k_cat, HEAD, 1)
        v_swap = pltpu.roll(v_cat, HEAD, 1)
        k_ops, v_blk = {}, {}
        for kv_head in range(ATT_KV_WIDTH // HEAD):
            for parity in range(2):
                own_lanes = kv_lane0 if parity == 0 else ~kv_lane0
                k_ops[kv_head, parity] = jnp.where(own_lanes, k_cat if kv_head == parity else k_swap, 0.0)
                v_blk[kv_head, parity] = jnp.where(own_lanes, v_cat if kv_head == parity else v_swap, 1.0)
        for hd in range(n_att_heads):
            kv_head, parity, pair = hd // ATT_GROUP, hd % 2, hd // 2
            scores.append(_mm(q[rows, pair * LANES:(pair + 1) * LANES], k_ops[kv_head, parity], tb=True))
            v_ops.append(v_blk[kv_head, parity])
        k_prev, v_prev = k_att[rows], v_att[rows]
    k_prev_ref[...] = k_prev
    v_prev_ref[...] = v_prev

    deferred.run(tile_steps)

    small = _Deferred()
    outs, sink_terms = {}, {}

    def softmax_pv(i):
        blk, hd = divmod(i, n_att_heads)
        valid = band & ((kj >= ATT_BLOCK) | (t_blk > 0)) if blk == 0 else band
        s = jnp.where(valid, scores[i], NEG_INF)
        sink = sinks_ref[hd]
        m = jnp.maximum(jnp.max(s, axis=-1, keepdims=True), sink)
        outs[i] = _mm(jnp.exp(s - m), v_ops[i])
        sink_terms[i] = jnp.exp(sink - m)

    for i in range(len(scores)):
        small.add(functools.partial(softmax_pv, i), ATT_BLOCK // 16)

    v = token_shift("v")
    k = k * (1.0 + (a - 1.0) * k_a_ref[...])
    bonus = head_sum(r * k * r_k_ref[...])
    kk = kk * jnp.minimum(lax.rsqrt(kk_norm2), 1.0 / L2_EPS)
    alpha = -(a * kk)
    deferred.run(2 * tile_steps)
    slabs, ptots = [], []
    for c in range(n_chunks):
        rows = slice(c * CHUNK, (c + 1) * CHUNK)
        cum = cums[c]
        total = cum[CHUNK - 1:CHUNK, :]
        p_incl = jnp.exp(cum)
        p_inv = jnp.exp(-cum)
        p_tail = jnp.exp(total - cum)
        full = (r[rows] * p_incl, kk[rows] * jnp.exp(cum - log_w[rows]), alpha[rows] * p_inv, k[rows] * p_inv,
                alpha[rows] * p_tail, k[rows] * p_tail, v[rows])
        ptot = jnp.exp(total)
        for gr in range(n_groups):
            ln = slice(gr * SLAB, (gr + 1) * SLAB)
            slabs.append(tuple(z[:, ln] for z in full))
            ptots.append(ptot[:, ln])
    y_slabs = _rwkv_chunks(slabs, ptots, n_groups, state_ref,
                           functools.partial(small.run, 2 * ATT_BLOCK // 16), functools.partial(deferred.run, tile_steps))
    y = cat([cat(y_slabs[c * n_groups:(c + 1) * n_groups], axis=1) for c in range(n_chunks)], axis=0)
    small.drain()
    deferred.drain()

    mean = head_sum(y) * (1.0 / HEAD)
    yc = y - mean

    att_blocks = []
    for blk in range(n_att_blocks):
        pair_slabs = []
        for pair in range(ATT_WIDTH // LANES):
            i0 = blk * n_att_heads + 2 * pair
            o0, o1 = outs[i0], outs[i0 + 1]
            num = jnp.where(out_lane0, o0, o1)
            den = pltpu.roll(jnp.where(out_lane0, o1, o0), HEAD, 1)
            den = den + jnp.where(out_lane0, sink_terms[i0], sink_terms[i0 + 1])
            pair_slabs.append(num / den)
        att_blocks.append(cat(pair_slabs, axis=1))
    y_att = cat(att_blocks, axis=0)

    g_att = proj("g_att")
    br_att = _mm(y_att * (g_att * _sigmoid(g_att)), w_br_att_ref[...])
    var = head_sum(yc * yc) * (1.0 / HEAD)
    y = yc * lax.rsqrt(var + GN_EPS) * gn_w_ref[...] + gn_b_ref[...]
    y_rwkv = y + bonus * v
    g_rwkv = proj("g_rwkv")
    br_rwkv = _mm(y_rwkv * (g_rwkv * _sigmoid(g_rwkv)), w_br_rwkv_ref[...])
    merged = _sigmoid(proj("gate_rwkv")) * br_rwkv + _sigmoid(proj("gate_att")) * br_att
    resid = x + _mm(merged, w_out_ref[...])
    out_ref[0] = _rms_norm(resid, g_final_ref[...])


def _const_spec(shape):
    return pl.BlockSpec(shape, lambda b, t: (0,) * len(shape), pipeline_mode=pl.Buffered(1))


def kernel(x, g_pre, w_in, mu_shift, w0, w_decay_up, a0, w_iclr_up, k_k, k_a, r_k, gn_w, gn_b, b_qkv,
           sinks, w_branch_rwkv, w_branch_att, w_out, g_final):
    assert g_pre.shape[0] == 1, "single layer"
    batch, seq, d = x.shape
    assert d == D_MODEL and seq % TOKENS_PER_STEP == 0 and w_in.shape[-1] == IN_COLS
    row = lambda p: p.reshape(1, -1).astype(F32)
    w_lora = jnp.zeros((2 * LORA, 2 * RWKV_WIDTH), F32)
    w_lora = w_lora.at[:LORA, :RWKV_WIDTH].set(w_decay_up[0]).at[LORA:, RWKV_WIDTH:].set(w_iclr_up[0])
    lane_head = jnp.arange(MXU_TILE) // HEAD
    head_ones = (lane_head[:, None] == lane_head[None, :]).astype(BF16)
    tri = (jnp.arange(CHUNK)[:, None] >= jnp.arange(CHUNK)[None, :]).astype(BF16)
    operands = [
        x, row(g_pre[0]), w_in[0].astype(BF16), row(mu_shift[0]),
        row(w0[0]), w_lora.astype(BF16), row(a0[0]),
        row(k_k[0]), row(k_a[0]), row(r_k[0]), row(gn_w[0]), row(gn_b[0]), row(b_qkv[0]),
        w_branch_rwkv[0].astype(BF16), w_branch_att[0].astype(BF16), w_out[0].astype(BF16), row(g_final),
        head_ones, tri,
    ]
    x_spec = pl.BlockSpec((1, TOKENS_PER_STEP, D_MODEL), lambda b, t: (b, t, 0))
    in_specs = [pl.BlockSpec(memory_space=pltpu.SMEM), x_spec] + [_const_spec(op.shape) for op in operands[1:]]
    return pl.pallas_call(
        _layer_kernel,
        grid=(batch, seq // TOKENS_PER_STEP),
        in_specs=in_specs,
        out_specs=x_spec,
        out_shape=jax.ShapeDtypeStruct(x.shape, x.dtype),
        scratch_shapes=[
            pltpu.VMEM((RWKV_WIDTH // SLAB, SLAB, SLAB), F32),
            pltpu.VMEM((8, SHIFT_WIDTH), F32),
            pltpu.VMEM((ATT_BLOCK, ATT_KV_WIDTH), F32),
            pltpu.VMEM((ATT_BLOCK, ATT_KV_WIDTH), F32),
        ],
        compiler_params=pltpu.CompilerParams(
            dimension_semantics=("arbitrary", "arbitrary"),
            vmem_limit_bytes=VMEM_LIMIT_BYTES,
        ),
        name="hybrid_rwkv7_swa_layer",
    )(sinks[0].astype(F32), *operands)
```

```python
import functools
import math

import jax
import jax.numpy as jnp
from jax import lax
from jax.experimental import pallas as pl
from jax.experimental.pallas import tpu as pltpu

D_MODEL = 1024
HEAD = 64
HEAD_SHIFT = 6
RWKV_WIDTH = 512
LORA = 64
ATT_WIDTH = 512
ATT_KV_WIDTH = 128
ATT_GROUP = 4
ATT_BLOCK = 128
QKV_WIDTH = ATT_WIDTH + 2 * ATT_KV_WIDTH
RKV_WIDTH = 3 * RWKV_WIDTH
SHIFT_WIDTH = RKV_WIDTH + 2 * LORA
PROJ_COLS = {"r": (0, RWKV_WIDTH), "k": (RWKV_WIDTH, 2 * RWKV_WIDTH), "v": (2 * RWKV_WIDTH, RKV_WIDTH),
             "lo": (RKV_WIDTH, SHIFT_WIDTH)}
_off = SHIFT_WIDTH
for _name, _width in (("g_rwkv", RWKV_WIDTH), ("qkv", QKV_WIDTH), ("g_att", ATT_WIDTH),
                      ("gate_rwkv", D_MODEL), ("gate_att", D_MODEL)):
    PROJ_COLS[_name] = (_off, _off + _width)
    _off += _width
IN_COLS = _off
RMS_EPS = 1e-6
GN_EPS = 64e-5
L2_EPS = 1e-12
NEG_INF = -1e30

LANES = 128
MXU_TILE = 256
SLAB = 128
TOKENS_PER_STEP = 512
CHUNK = 64
SQUARINGS = int(math.log2(CHUNK))
VMEM_LIMIT_BYTES = 56 * 1024 * 1024

F32 = jnp.float32
BF16 = jnp.bfloat16


def _mm(a, b, *, tb=False, passes=1):
    dn = (((1,), (1 if tb else 0,)), ((), ()))
    dot = lambda x, y: lax.dot_general(x, y, dn, preferred_element_type=F32)
    a_hi = a.astype(BF16)
    b_hi = b.astype(BF16)
    out = dot(a_hi, b_hi)
    if passes >= 2:
        a_lo = (a - a_hi.astype(F32)).astype(BF16)
        out = out + dot(a_lo, b_hi)
    if passes >= 3:
        b_lo = (b - b_hi.astype(F32)).astype(BF16)
        out = out + dot(a_hi, b_lo)
    return out


def _sigmoid(x):
    return 1.0 / (1.0 + jnp.exp(-x))


def _rms_norm(x, g):
    return x * lax.rsqrt(jnp.mean(x * x, axis=-1, keepdims=True) + RMS_EPS) * g


def _iota(shape, dim):
    return lax.broadcasted_iota(jnp.int32, shape, dim)


def _block_diag(z, mask):
    return jnp.where(mask, jnp.concatenate([z] * (SLAB // CHUNK), axis=0), 0.0)


class _Deferred:
    def __init__(self):
        self._queue = []

    def add(self, thunk, cost):
        self._queue.append((thunk, cost))

    def run(self, budget):
        while self._queue and budget > 0:
            thunk, cost = self._queue.pop(0)
            thunk()
            budget -= cost

    def drain(self):
        self.run(float("inf"))


def _rwkv_chunks(slabs, ptots, n_groups, state_ref, fill_wave, fill_chain):
    heads = SLAB // HEAD
    row = _iota((CHUNK, SLAB), 0)
    col = _iota((CHUNK, SLAB), 1)
    src = col & (HEAD - 1)
    strict = src < row
    incl = src <= row
    eye = src == row
    lane_head = col >> HEAD_SHIFT
    r2 = _iota((SLAB, SLAB), 0)
    c2 = _iota((SLAB, SLAB), 1)
    bd_mask = (r2 >> HEAD_SHIFT) == (c2 >> HEAD_SHIFT)
    bd = functools.partial(_block_diag, mask=bd_mask)
    cat = jnp.concatenate
    n = len(slabs)

    a_all = []
    for rt, bt, at, kt, _, _, _ in slabs:
        rhs_t = cat([jnp.where(lane_head == hd, z, 0.0) for z in (at, kt) for hd in range(heads)], axis=0)
        a_all.append(_mm(cat([bt, rt], axis=0), rhs_t, tb=True))
    a_ba = [jnp.where(strict, a[:CHUNK, :SLAB], 0.0) for a in a_all]
    a_bk = [jnp.where(strict, a[:CHUNK, SLAB:], 0.0) for a in a_all]
    a_ra = [jnp.where(incl, a[CHUNK:, :SLAB], 0.0) for a in a_all]
    a_rk = [jnp.where(incl, a[CHUNK:, SLAB:], 0.0) for a in a_all]
    fill_wave()

    p = [_mm(a, bd(a)) for a in a_ba]
    av = [_mm(cat([a_bk[i], a_rk[i]], axis=0), bd(slabs[i][6])) for i in range(n)]
    fill_wave()

    x = [jnp.where(eye, 1.0, 0.0) + a for a in a_ba]
    for level in range(2, SQUARINGS + 1):
        if level < SQUARINGS:
            prod = [_mm(cat([x[i], p[i]], axis=0), bd(p[i])) for i in range(n)]
            x = [x[i] + prod[i][:CHUNK] for i in range(n)]
            p = [prod[i][CHUNK:] for i in range(n)]
        else:
            x = [x[i] + _mm(x[i], bd(p[i])) for i in range(n)]
        fill_wave()

    gw = [_mm(x[i], cat([bd(slabs[i][1]), bd(av[i][:CHUNK])], axis=1)) for i in range(n)]
    fill_wave()
    rg = [_mm(a_ra[i], cat([bd(gw[i][:, :SLAB]), bd(gw[i][:, SLAB:])], axis=1)) for i in range(n)]
    r_hat = [slabs[i][0] + rg[i][:, :SLAB] for i in range(n)]
    y0 = [rg[i][:, SLAB:] + av[i][CHUNK:] for i in range(n)]

    tiles = SLAB // LANES
    mn = []
    for i in range(n):
        _, _, _, _, ah, kh, v = slabs[i]
        for tl in range(tiles):
            ln = slice(tl * LANES, (tl + 1) * LANES)
            z_t = cat([ah[:, ln], kh[:, ln]], axis=0).T
            rhs = cat([cat([gw[i][:, ln], gw[i][:, SLAB + tl * LANES:SLAB + (tl + 1) * LANES]], axis=1),
                       cat([jnp.zeros_like(v[:, ln]), v[:, ln]], axis=1)], axis=0)
            mn.append(_mm(z_t, rhs))
    fill_wave()
    zero = jnp.zeros((LANES, LANES), F32)
    tile_diag = lambda blocks: cat([cat([blk if j == tl else zero for j in range(tiles)], axis=1)
                                    for tl, blk in enumerate(blocks)], axis=0)
    m_bd, n_bd = [], []
    for i in range(n):
        parts = mn[i * tiles:(i + 1) * tiles]
        m_full = tile_diag([q[:, :LANES] for q in parts])
        n_full = tile_diag([q[:, LANES:] for q in parts])
        m_bd.append(jnp.where(bd_mask, m_full, 0.0) + jnp.where(r2 == c2, ptots[i], 0.0))
        n_bd.append(jnp.where(bd_mask, n_full, 0.0))

    state = [state_ref[gr] for gr in range(n_groups)]
    ys = []
    for i in range(n):
        gr = i % n_groups
        out = _mm(cat([r_hat[i], m_bd[i]], axis=0), state[gr])
        ys.append(out[:CHUNK] + y0[i])
        state[gr] = out[CHUNK:] + n_bd[i]
        if gr == n_groups - 1:
            fill_chain()
    for gr in range(n_groups):
        state_ref[gr] = state[gr]
    return ys


def _layer_kernel(sinks_ref, x_ref, g_pre_ref, w_in_ref, mu_ref, w0_ref, w_lora_ref, a0_ref,
                  k_k_ref, k_a_ref, r_k_ref, gn_w_ref, gn_b_ref, b_qkv_ref, w_br_rwkv_ref, w_br_att_ref,
                  w_out_ref, g_final_ref, head_ones_ref, tri_ref, out_ref,
                  state_ref, tail_ref, k_prev_ref, v_prev_ref):
    tb = TOKENS_PER_STEP
    t_blk = pl.program_id(1)

    @pl.when(t_blk == 0)
    def _():
        state_ref[...] = jnp.zeros_like(state_ref)
        tail_ref[...] = jnp.zeros_like(tail_ref)
        k_prev_ref[...] = jnp.zeros_like(k_prev_ref)
        v_prev_ref[...] = jnp.zeros_like(v_prev_ref)

    cat = jnp.concatenate
    x = x_ref[0]
    h = _rms_norm(x, g_pre_ref[...]).astype(BF16)

    tiles = {}

    def project_tile(j):
        tiles[j] = jnp.dot(h, w_in_ref[:, j * MXU_TILE:min((j + 1) * MXU_TILE, IN_COLS)], preferred_element_type=F32)

    def proj(name):
        lo, hi = PROJ_COLS[name]
        return cat([tiles[c // MXU_TILE][:, c % MXU_TILE:c % MXU_TILE + LANES] for c in range(lo, hi, LANES)], axis=1)

    def tiles_of(*names):
        return sorted({c // MXU_TILE for nm in names for c in range(*PROJ_COLS[nm], LANES)})

    def project(*names):
        for j in tiles_of(*names):
            if j not in tiles:
                project_tile(j)

    def token_shift(name):
        p = proj(name)
        cols = slice(PROJ_COLS[name][0], PROJ_COLS[name][1])
        prev = pltpu.roll(p, 1, 0)
        prev = jnp.where(_iota(p.shape, 0) == 0, tail_ref[7:8, cols], prev)
        tail_ref[:, cols] = p[tb - 8:, :]
        return p + (prev - p) * mu_ref[:, cols]

    head_ones = head_ones_ref[...]
    head_sum = lambda z: cat([_mm(z[:, j:j + MXU_TILE], head_ones) for j in range(0, RWKV_WIDTH, MXU_TILE)], axis=1)

    project("lo")
    project("r", "k")
    lo = token_shift("lo")
    lo = jnp.where(_iota(lo.shape, 1) < LORA, jnp.tanh(lo), lo)
    lora = _mm(lo, w_lora_ref[...])
    project("v")
    r = token_shift("r")
    k = token_shift("k")
    kk = k * k_k_ref[...]
    kk_norm2 = head_sum(kk * kk)
    log_w = -math.exp(-0.5) * _sigmoid(w0_ref[...] + lora[:, :RWKV_WIDTH])
    a = _sigmoid(a0_ref[...] + lora[:, RWKV_WIDTH:])
    n_groups = RWKV_WIDTH // SLAB
    n_chunks = tb // CHUNK
    tri = tri_ref[...]
    cums = []
    for c in range(n_chunks):
        lw = log_w[c * CHUNK:(c + 1) * CHUNK]
        lw_hi = lw.astype(BF16)
        lw_mid = (lw - lw_hi.astype(F32)).astype(BF16)
        lw_lo = (lw - lw_hi.astype(F32) - lw_mid.astype(F32)).astype(BF16)
        cums.append(jnp.dot(tri, lw_hi, preferred_element_type=F32)
                    + jnp.dot(tri, lw_mid, preferred_element_type=F32)
                    + jnp.dot(tri, lw_lo, preferred_element_type=F32))
    project("qkv")

    tile_steps = tb // 16
    deferred = _Deferred()
    for j in range(-(-IN_COLS // MXU_TILE)):
        if j not in tiles:
            deferred.add(functools.partial(project_tile, j), tile_steps)
    deferred.run(tile_steps)
    qkv = proj("qkv") + b_qkv_ref[...]

    q = qkv[:, :ATT_WIDTH] * (HEAD ** -0.5)
    k_att = qkv[:, ATT_WIDTH:ATT_WIDTH + ATT_KV_WIDTH]
    v_att = qkv[:, ATT_WIDTH + ATT_KV_WIDTH:]
    qi = _iota((ATT_BLOCK, 2 * ATT_BLOCK), 0)
    kj = _iota((ATT_BLOCK, 2 * ATT_BLOCK), 1)
    band = (kj > qi) & (kj <= qi + ATT_BLOCK)
    kv_lane0 = _iota((2 * ATT_BLOCK, LANES), 1) < HEAD
    out_lane0 = _iota((ATT_BLOCK, LANES), 1) < HEAD
    n_att_heads = ATT_WIDTH // HEAD
    n_att_blocks = tb // ATT_BLOCK
    k_prev = k_prev_ref[...]
    v_prev = v_prev_ref[...]
    scores, v_ops = [], []
    for blk in range(n_att_blocks):
        rows = slice(blk * ATT_BLOCK, (blk + 1) * ATT_BLOCK)
        k_cat = cat([k_prev, k_att[rows]], axis=0)
        v_cat = cat([v_prev, v_att[rows]], axis=0)
        k_swap = pltpu.roll(k_cat, HEAD, 1)
        v_swap = pltpu.roll(v_cat, HEAD, 1)
        k_ops, v_blk = {}, {}
        for kv_head in range(ATT_KV_WIDTH // HEAD):
            for parity in range(2):
                own_lanes = kv_lane0 if parity == 0 else ~kv_lane0
                k_ops[kv_head, parity] = jnp.where(own_lanes, k_cat if kv_head == parity else k_swap, 0.0)
                v_blk[kv_head, parity] = jnp.where(own_lanes, v_cat if kv_head == parity else v_swap, 1.0)
        for hd in range(n_att_heads):
            kv_head, parity, pair = hd // ATT_GROUP, hd % 2, hd // 2
            scores.append(_mm(q[rows, pair * LANES:(pair + 1) * LANES], k_ops[kv_head, parity], tb=True))
            v_ops.append(v_blk[kv_head, parity])
        k_prev, v_prev = k_att[rows], v_att[rows]
    k_prev_ref[...] = k_prev
    v_prev_ref[...] = v_prev

    deferred.run(tile_steps)

    small = _Deferred()
    outs, sink_terms = {}, {}

    def softmax_pv(i):
        blk, hd = divmod(i, n_att_heads)
        valid = band & ((kj >= ATT_BLOCK) | (t_blk > 0)) if blk == 0 else band
        s = jnp.where(valid, scores[i], NEG_INF)
        sink = sinks_ref[hd]
        m = jnp.maximum(jnp.max(s, axis=-1, keepdims=True), sink)
        outs[i] = _mm(jnp.exp(s - m), v_ops[i])
        sink_terms[i] = jnp.exp(sink - m)

    for i in range(len(scores)):
        small.add(functools.partial(softmax_pv, i), ATT_BLOCK // 16)

    v = token_shift("v")
    k = k * (1.0 + (a - 1.0) * k_a_ref[...])
    bonus = head_sum(r * k * r_k_ref[...])
    kk = kk * jnp.minimum(lax.rsqrt(kk_norm2), 1.0 / L2_EPS)
    alpha = -(a * kk)
    deferred.run(2 * tile_steps)
    slabs, ptots = [], []
    for c in range(n_chunks):
        rows = slice(c * CHUNK, (c + 1) * CHUNK)
        cum = cums[c]
        total = cum[CHUNK - 1:CHUNK, :]
        p_incl = jnp.exp(cum)
        p_inv = jnp.exp(-cum)
        p_tail = jnp.exp(total - cum)
        full = (r[rows] * p_incl, kk[rows] * jnp.exp(cum - log_w[rows]), alpha[rows] * p_inv, k[rows] * p_inv,
                alpha[rows] * p_tail, k[rows] * p_tail, v[rows])
        ptot = jnp.exp(total)
        for gr in range(n_groups):
            ln = slice(gr * SLAB, (gr + 1) * SLAB)
            slabs.append(tuple(z[:, ln] for z in full))
            ptots.append(ptot[:, ln])
    y_slabs = _rwkv_chunks(slabs, ptots, n_groups, state_ref,
                           functools.partial(small.run, len(scores) // 8 * (ATT_BLOCK // 16)),
                           functools.partial(deferred.run, tile_steps))
    y = cat([cat(y_slabs[c * n_groups:(c + 1) * n_groups], axis=1) for c in range(n_chunks)], axis=0)
    small.drain()
    deferred.drain()

    mean = head_sum(y) * (1.0 / HEAD)
    yc = y - mean

    att_blocks = []
    for blk in range(n_att_blocks):
        pair_slabs = []
        for pair in range(ATT_WIDTH // LANES):
            i0 = blk * n_att_heads + 2 * pair
            o0, o1 = outs[i0], outs[i0 + 1]
            num = jnp.where(out_lane0, o0, o1)
            den = pltpu.roll(jnp.where(out_lane0, o1, o0), HEAD, 1)
            den = den + jnp.where(out_lane0, sink_terms[i0], sink_terms[i0 + 1])
            pair_slabs.append(num / den)
        att_blocks.append(cat(pair_slabs, axis=1))
    y_att = cat(att_blocks, axis=0)

    g_att = proj("g_att")
    br_att = _mm(y_att * (g_att * _sigmoid(g_att)), w_br_att_ref[...])
    var = head_sum(yc * yc) * (1.0 / HEAD)
    y = yc * lax.rsqrt(var + GN_EPS) * gn_w_ref[...] + gn_b_ref[...]
    y_rwkv = y + bonus * v
    g_rwkv = proj("g_rwkv")
    br_rwkv = _mm(y_rwkv * (g_rwkv * _sigmoid(g_rwkv)), w_br_rwkv_ref[...])
    merged = _sigmoid(proj("gate_rwkv")) * br_rwkv + _sigmoid(proj("gate_att")) * br_att
    resid = x + _mm(merged, w_out_ref[...])
    out_ref[0] = _rms_norm(resid, g_final_ref[...])


def _const_spec(shape):
    return pl.BlockSpec(shape, lambda b, t: (0,) * len(shape), pipeline_mode=pl.Buffered(1))


def kernel(x, g_pre, w_in, mu_shift, w0, w_decay_up, a0, w_iclr_up, k_k, k_a, r_k, gn_w, gn_b, b_qkv,
           sinks, w_branch_rwkv, w_branch_att, w_out, g_final):
    assert g_pre.shape[0] == 1, "single layer"
    batch, seq, d = x.shape
    assert d == D_MODEL and seq % TOKENS_PER_STEP == 0 and w_in.shape[-1] == IN_COLS
    row = lambda p: p.reshape(1, -1).astype(F32)
    w_lora = jnp.zeros((2 * LORA, 2 * RWKV_WIDTH), F32)
    w_lora = w_lora.at[:LORA, :RWKV_WIDTH].set(w_decay_up[0]).at[LORA:, RWKV_WIDTH:].set(w_iclr_up[0])
    lane_head = jnp.arange(MXU_TILE) // HEAD
    head_ones = (lane_head[:, None] == lane_head[None, :]).astype(BF16)
    tri = (jnp.arange(CHUNK)[:, None] >= jnp.arange(CHUNK)[None, :]).astype(BF16)
    operands = [
        x, row(g_pre[0]), w_in[0].astype(BF16), row(mu_shift[0]),
        row(w0[0]), w_lora.astype(BF16), row(a0[0]),
        row(k_k[0]), row(k_a[0]), row(r_k[0]), row(gn_w[0]), row(gn_b[0]), row(b_qkv[0]),
        w_branch_rwkv[0].astype(BF16), w_branch_att[0].astype(BF16), w_out[0].astype(BF16), row(g_final),
        head_ones, tri,
    ]
    x_spec = pl.BlockSpec((1, TOKENS_PER_STEP, D_MODEL), lambda b, t: (b, t, 0))
    in_specs = [pl.BlockSpec(memory_space=pltpu.SMEM), x_spec] + [_const_spec(op.shape) for op in operands[1:]]
    return pl.pallas_call(
        _layer_kernel,
        grid=(batch, seq // TOKENS_PER_STEP),
        in_specs=in_specs,
        out_specs=x_spec,
        out_shape=jax.ShapeDtypeStruct(x.shape, x.dtype),
        scratch_shapes=[
            pltpu.VMEM((RWKV_WIDTH // SLAB, SLAB, SLAB), F32),
            pltpu.VMEM((8, SHIFT_WIDTH), F32),
            pltpu.VMEM((ATT_BLOCK, ATT_KV_WIDTH), F32),
            pltpu.VMEM((ATT_BLOCK, ATT_KV_WIDTH), F32),
        ],
        compiler_params=pltpu.CompilerParams(
            dimension_semantics=("arbitrary", "arbitrary"),
            vmem_limit_bytes=VMEM_LIMIT_BYTES,
        ),
        name="hybrid_rwkv7_swa_layer",
    )(sinks[0].astype(F32), *operands)
```

```python
import functools
import math

import jax
import jax.numpy as jnp
from jax import lax
from jax.experimental import pallas as pl
from jax.experimental.pallas import tpu as pltpu

D_MODEL = 1024
HEAD = 64
HEAD_SHIFT = 6
RWKV_WIDTH = 512
LORA = 64
ATT_WIDTH = 512
ATT_KV_WIDTH = 128
ATT_GROUP = 4
ATT_BLOCK = 128
QKV_WIDTH = ATT_WIDTH + 2 * ATT_KV_WIDTH
RKV_WIDTH = 3 * RWKV_WIDTH
SHIFT_WIDTH = RKV_WIDTH + 2 * LORA
PROJ_COLS = {"r": (0, RWKV_WIDTH), "k": (RWKV_WIDTH, 2 * RWKV_WIDTH), "v": (2 * RWKV_WIDTH, RKV_WIDTH),
             "lo": (RKV_WIDTH, SHIFT_WIDTH)}
_off = SHIFT_WIDTH
for _name, _width in (("g_rwkv", RWKV_WIDTH), ("qkv", QKV_WIDTH), ("g_att", ATT_WIDTH),
                      ("gate_rwkv", D_MODEL), ("gate_att", D_MODEL)):
    PROJ_COLS[_name] = (_off, _off + _width)
    _off += _width
IN_COLS = _off
RMS_EPS = 1e-6
GN_EPS = 64e-5
L2_EPS = 1e-12
NEG_INF = -1e30

LANES = 128
MXU_TILE = 256
SLAB = 128
TOKENS_PER_STEP = 512
CHUNK = 64
SQUARINGS = int(math.log2(CHUNK))
VMEM_LIMIT_BYTES = 56 * 1024 * 1024
W_IN_COPY_ROWS = 128

F32 = jnp.float32
BF16 = jnp.bfloat16


def _mm(a, b, *, tb=False, passes=1):
    dn = (((1,), (1 if tb else 0,)), ((), ()))
    dot = lambda x, y: lax.dot_general(x, y, dn, preferred_element_type=F32)
    a_hi = a.astype(BF16)
    b_hi = b.astype(BF16)
    out = dot(a_hi, b_hi)
    if passes >= 2:
        a_lo = (a - a_hi.astype(F32)).astype(BF16)
        out = out + dot(a_lo, b_hi)
    if passes >= 3:
        b_lo = (b - b_hi.astype(F32)).astype(BF16)
        out = out + dot(a_hi, b_lo)
    return out


def _sigmoid(x):
    return 1.0 / (1.0 + jnp.exp(-x))


def _rms_norm(x, g):
    return x * lax.rsqrt(jnp.mean(x * x, axis=-1, keepdims=True) + RMS_EPS) * g


def _iota(shape, dim):
    return lax.broadcasted_iota(jnp.int32, shape, dim)


def _block_diag(z, mask):
    return jnp.where(mask, jnp.concatenate([z] * (SLAB // CHUNK), axis=0), 0.0)


class _Deferred:
    def __init__(self):
        self._queue = []

    def add(self, thunk, cost):
        self._queue.append((thunk, cost))

    def run(self, budget):
        while self._queue and budget > 0:
            thunk, cost = self._queue.pop(0)
            thunk()
            budget -= cost

    def drain(self):
        self.run(float("inf"))


def _rwkv_chunks(slabs, ptots, n_groups, state_ref, fill_wave, fill_chain):
    heads = SLAB // HEAD
    row = _iota((CHUNK, SLAB), 0)
    col = _iota((CHUNK, SLAB), 1)
    src = col & (HEAD - 1)
    strict = src < row
    incl = src <= row
    eye = src == row
    lane_head = col >> HEAD_SHIFT
    r2 = _iota((SLAB, SLAB), 0)
    c2 = _iota((SLAB, SLAB), 1)
    bd_mask = (r2 >> HEAD_SHIFT) == (c2 >> HEAD_SHIFT)
    bd = functools.partial(_block_diag, mask=bd_mask)
    cat = jnp.concatenate
    n = len(slabs)

    a_all = []
    for rt, bt, at, kt, _, _, _ in slabs:
        rhs_t = cat([jnp.where(lane_head == hd, z, 0.0) for z in (at, kt) for hd in range(heads)], axis=0)
        a_all.append(_mm(cat([bt, rt], axis=0), rhs_t, tb=True))
    a_ba = [jnp.where(strict, a[:CHUNK, :SLAB], 0.0) for a in a_all]
    a_bk = [jnp.where(strict, a[:CHUNK, SLAB:], 0.0) for a in a_all]
    a_ra = [jnp.where(incl, a[CHUNK:, :SLAB], 0.0) for a in a_all]
    a_rk = [jnp.where(incl, a[CHUNK:, SLAB:], 0.0) for a in a_all]
    fill_wave()

    p = [_mm(a, bd(a)) for a in a_ba]
    av = [_mm(cat([a_bk[i], a_rk[i]], axis=0), bd(slabs[i][6])) for i in range(n)]
    fill_wave()

    x = [jnp.where(eye, 1.0, 0.0) + a for a in a_ba]
    for level in range(2, SQUARINGS + 1):
        if level < SQUARINGS:
            prod = [_mm(cat([x[i], p[i]], axis=0), bd(p[i])) for i in range(n)]
            x = [x[i] + prod[i][:CHUNK] for i in range(n)]
            p = [prod[i][CHUNK:] for i in range(n)]
        else:
            x = [x[i] + _mm(x[i], bd(p[i])) for i in range(n)]
        fill_wave()

    gw = [_mm(x[i], cat([bd(slabs[i][1]), bd(av[i][:CHUNK])], axis=1)) for i in range(n)]
    fill_wave()
    rg = [_mm(a_ra[i], cat([bd(gw[i][:, :SLAB]), bd(gw[i][:, SLAB:])], axis=1)) for i in range(n)]
    r_hat = [slabs[i][0] + rg[i][:, :SLAB] for i in range(n)]
    y0 = [rg[i][:, SLAB:] + av[i][CHUNK:] for i in range(n)]

    tiles = SLAB // LANES
    mn = []
    for i in range(n):
        _, _, _, _, ah, kh, v = slabs[i]
        for tl in range(tiles):
            ln = slice(tl * LANES, (tl + 1) * LANES)
            z_t = cat([ah[:, ln], kh[:, ln]], axis=0).T
            rhs = cat([cat([gw[i][:, ln], gw[i][:, SLAB + tl * LANES:SLAB + (tl + 1) * LANES]], axis=1),
                       cat([jnp.zeros_like(v[:, ln]), v[:, ln]], axis=1)], axis=0)
            mn.append(_mm(z_t, rhs))
    fill_wave()
    zero = jnp.zeros((LANES, LANES), F32)
    tile_diag = lambda blocks: cat([cat([blk if j == tl else zero for j in range(tiles)], axis=1)
                                    for tl, blk in enumerate(blocks)], axis=0)
    m_bd, n_bd = [], []
    for i in range(n):
        parts = mn[i * tiles:(i + 1) * tiles]
        m_full = tile_diag([q[:, :LANES] for q in parts])
        n_full = tile_diag([q[:, LANES:] for q in parts])
        m_bd.append(jnp.where(bd_mask, m_full, 0.0) + jnp.where(r2 == c2, ptots[i], 0.0))
        n_bd.append(jnp.where(bd_mask, n_full, 0.0))

    state = [state_ref[gr] for gr in range(n_groups)]
    ys = []
    for i in range(n):
        gr = i % n_groups
        out = _mm(cat([r_hat[i], m_bd[i]], axis=0), state[gr])
        ys.append(out[:CHUNK] + y0[i])
        state[gr] = out[CHUNK:] + n_bd[i]
        if gr == n_groups - 1:
            fill_chain()
    for gr in range(n_groups):
        state_ref[gr] = state[gr]
    return ys


def _load_w_in(w_in_hbm_ref, w_in_ref, stage_ref, stage_sem):
    n_copies = D_MODEL // W_IN_COPY_ROWS

    def copy(i):
        rows = pl.ds(i * W_IN_COPY_ROWS, W_IN_COPY_ROWS)
        return pltpu.make_async_copy(w_in_hbm_ref.at[rows, :], stage_ref.at[i % 2], stage_sem.at[i % 2])

    copy(0).start()
    copy(1).start()
    for i in range(n_copies):
        copy(i).wait()
        w_in_ref[i * W_IN_COPY_ROWS:(i + 1) * W_IN_COPY_ROWS, :] = stage_ref[i % 2].astype(BF16)
        if i + 2 < n_copies:
            copy(i + 2).start()


def _layer_kernel(sinks_ref, x_ref, g_pre_ref, w_in_hbm_ref, mu_ref, w0_ref, w_lora_ref, a0_ref,
                  k_k_ref, k_a_ref, r_k_ref, gn_w_ref, gn_b_ref, b_qkv_ref, w_br_rwkv_ref, w_br_att_ref,
                  w_out_ref, g_final_ref, head_ones_ref, tri_ref, out_ref,
                  w_in_ref, stage_ref, stage_sem, state_ref, tail_ref, k_prev_ref, v_prev_ref):
    tb = TOKENS_PER_STEP
    t_blk = pl.program_id(1)

    @pl.when((pl.program_id(0) == 0) & (t_blk == 0))
    def _():
        _load_w_in(w_in_hbm_ref, w_in_ref, stage_ref, stage_sem)

    @pl.when(t_blk == 0)
    def _():
        state_ref[...] = jnp.zeros_like(state_ref)
        tail_ref[...] = jnp.zeros_like(tail_ref)
        k_prev_ref[...] = jnp.zeros_like(k_prev_ref)
        v_prev_ref[...] = jnp.zeros_like(v_prev_ref)

    cat = jnp.concatenate
    x = x_ref[0]
    h = _rms_norm(x, g_pre_ref[...]).astype(BF16)

    tiles = {}

    def project_tile(j):
        tiles[j] = jnp.dot(h, w_in_ref[:, j * MXU_TILE:min((j + 1) * MXU_TILE, IN_COLS)], preferred_element_type=F32)

    def proj(name):
        lo, hi = PROJ_COLS[name]
        return cat([tiles[c // MXU_TILE][:, c % MXU_TILE:c % MXU_TILE + LANES] for c in range(lo, hi, LANES)], axis=1)

    def tiles_of(*names):
        return sorted({c // MXU_TILE for nm in names for c in range(*PROJ_COLS[nm], LANES)})

    def project(*names):
        for j in tiles_of(*names):
            if j not in tiles:
                project_tile(j)

    def token_shift(name):
        p = proj(name)
        cols = slice(PROJ_COLS[name][0], PROJ_COLS[name][1])
        prev = pltpu.roll(p, 1, 0)
        prev = jnp.where(_iota(p.shape, 0) == 0, tail_ref[7:8, cols], prev)
        tail_ref[:, cols] = p[tb - 8:, :]
        return p + (prev - p) * mu_ref[:, cols]

    head_ones = head_ones_ref[...]
    head_sum = lambda z: cat([_mm(z[:, j:j + MXU_TILE], head_ones) for j in range(0, RWKV_WIDTH, MXU_TILE)], axis=1)

    project("lo")
    project("r", "k")
    lo = token_shift("lo")
    lo = jnp.where(_iota(lo.shape, 1) < LORA, jnp.tanh(lo), lo)
    lora = _mm(lo, w_lora_ref[...])
    project("v")
    r = token_shift("r")
    k = token_shift("k")
    kk = k * k_k_ref[...]
    kk_norm2 = head_sum(kk * kk)
    log_w = -math.exp(-0.5) * _sigmoid(w0_ref[...] + lora[:, :RWKV_WIDTH])
    a = _sigmoid(a0_ref[...] + lora[:, RWKV_WIDTH:])
    n_groups = RWKV_WIDTH // SLAB
    n_chunks = tb // CHUNK
    tri = tri_ref[...]
    cums = []
    for c in range(n_chunks):
        lw = log_w[c * CHUNK:(c + 1) * CHUNK]
        lw_hi = lw.astype(BF16)
        lw_mid = (lw - lw_hi.astype(F32)).astype(BF16)
        lw_lo = (lw - lw_hi.astype(F32) - lw_mid.astype(F32)).astype(BF16)
        cums.append(jnp.dot(tri, lw_hi, preferred_element_type=F32)
                    + jnp.dot(tri, lw_mid, preferred_element_type=F32)
                    + jnp.dot(tri, lw_lo, preferred_element_type=F32))
    project("qkv")

    tile_steps = tb // 16
    deferred = _Deferred()
    for j in range(-(-IN_COLS // MXU_TILE)):
        if j not in tiles:
            deferred.add(functools.partial(project_tile, j), tile_steps)
    deferred.run(tile_steps)
    qkv = proj("qkv") + b_qkv_ref[...]

    q = qkv[:, :ATT_WIDTH] * (HEAD ** -0.5)
    k_att = qkv[:, ATT_WIDTH:ATT_WIDTH + ATT_KV_WIDTH]
    v_att = qkv[:, ATT_WIDTH + ATT_KV_WIDTH:]
    qi = _iota((ATT_BLOCK, 2 * ATT_BLOCK), 0)
    kj = _iota((ATT_BLOCK, 2 * ATT_BLOCK), 1)
    band = (kj > qi) & (kj <= qi + ATT_BLOCK)
    kv_lane0 = _iota((2 * ATT_BLOCK, LANES), 1) < HEAD
    out_lane0 = _iota((ATT_BLOCK, LANES), 1) < HEAD
    n_att_heads = ATT_WIDTH // HEAD
    n_att_blocks = tb // ATT_BLOCK
    k_prev = k_prev_ref[...]
    v_prev = v_prev_ref[...]
    scores, v_ops = [], []
    for blk in range(n_att_blocks):
        rows = slice(blk * ATT_BLOCK, (blk + 1) * ATT_BLOCK)
        k_cat = cat([k_prev, k_att[rows]], axis=0)
        v_cat = cat([v_prev, v_att[rows]], axis=0)
        k_swap = pltpu.roll(k_cat, HEAD, 1)
        v_swap = pltpu.roll(v_cat, HEAD, 1)
        k_ops, v_blk = {}, {}
        for kv_head in range(ATT_KV_WIDTH // HEAD):
            for parity in range(2):
                own_lanes = kv_lane0 if parity == 0 else ~kv_lane0
                k_ops[kv_head, parity] = jnp.where(own_lanes, k_cat if kv_head == parity else k_swap, 0.0)
                v_blk[kv_head, parity] = jnp.where(own_lanes, v_cat if kv_head == parity else v_swap, 1.0)
        for hd in range(n_att_heads):
            kv_head, parity, pair = hd // ATT_GROUP, hd % 2, hd // 2
            scores.append(_mm(q[rows, pair * LANES:(pair + 1) * LANES], k_ops[kv_head, parity], tb=True))
            v_ops.append(v_blk[kv_head, parity])
        k_prev, v_prev = k_att[rows], v_att[rows]
    k_prev_ref[...] = k_prev
    v_prev_ref[...] = v_prev

    deferred.run(tile_steps)

    small = _Deferred()
    outs, sink_terms = {}, {}

    def softmax_pv(i):
        blk, hd = divmod(i, n_att_heads)
        valid = band & ((kj >= ATT_BLOCK) | (t_blk > 0)) if blk == 0 else band
        s = jnp.where(valid, scores[i], NEG_INF)
        sink = sinks_ref[hd]
        m = jnp.maximum(jnp.max(s, axis=-1, keepdims=True), sink)
        outs[i] = _mm(jnp.exp(s - m), v_ops[i])
        sink_terms[i] = jnp.exp(sink - m)

    for i in range(len(scores)):
        small.add(functools.partial(softmax_pv, i), ATT_BLOCK // 16)

    v = token_shift("v")
    k = k * (1.0 + (a - 1.0) * k_a_ref[...])
    bonus = head_sum(r * k * r_k_ref[...])
    kk = kk * jnp.minimum(lax.rsqrt(kk_norm2), 1.0 / L2_EPS)
    alpha = -(a * kk)
    deferred.run(2 * tile_steps)
    slabs, ptots = [], []
    for c in range(n_chunks):
        rows = slice(c * CHUNK, (c + 1) * CHUNK)
        cum = cums[c]
        total = cum[CHUNK - 1:CHUNK, :]
        p_incl = jnp.exp(cum)
        p_inv = jnp.exp(-cum)
        p_tail = jnp.exp(total - cum)
        full = (r[rows] * p_incl, kk[rows] * jnp.exp(cum - log_w[rows]), alpha[rows] * p_inv, k[rows] * p_inv,
                alpha[rows] * p_tail, k[rows] * p_tail, v[rows])
        ptot = jnp.exp(total)
        for gr in range(n_groups):
            ln = slice(gr * SLAB, (gr + 1) * SLAB)
            slabs.append(tuple(z[:, ln] for z in full))
            ptots.append(ptot[:, ln])
    y_slabs = _rwkv_chunks(slabs, ptots, n_groups, state_ref,
                           functools.partial(small.run, len(scores) // 8 * (ATT_BLOCK // 16)),
                           functools.partial(deferred.run, tile_steps))
    y = cat([cat(y_slabs[c * n_groups:(c + 1) * n_groups], axis=1) for c in range(n_chunks)], axis=0)
    small.drain()
    deferred.drain()

    mean = head_sum(y) * (1.0 / HEAD)
    yc = y - mean

    att_blocks = []
    for blk in range(n_att_blocks):
        pair_slabs = []
        for pair in range(ATT_WIDTH // LANES):
            i0 = blk * n_att_heads + 2 * pair
            o0, o1 = outs[i0], outs[i0 + 1]
            num = jnp.where(out_lane0, o0, o1)
            den = pltpu.roll(jnp.where(out_lane0, o1, o0), HEAD, 1)
            den = den + jnp.where(out_lane0, sink_terms[i0], sink_terms[i0 + 1])
            pair_slabs.append(num / den)
        att_blocks.append(cat(pair_slabs, axis=1))
    y_att = cat(att_blocks, axis=0)

    g_att = proj("g_att")
    br_att = _mm(y_att * (g_att * _sigmoid(g_att)), w_br_att_ref[...])
    var = head_sum(yc * yc) * (1.0 / HEAD)
    y = yc * lax.rsqrt(var + GN_EPS) * gn_w_ref[...] + gn_b_ref[...]
    y_rwkv = y + bonus * v
    g_rwkv = proj("g_rwkv")
    br_rwkv = _mm(y_rwkv * (g_rwkv * _sigmoid(g_rwkv)), w_br_rwkv_ref[...])
    merged = _sigmoid(proj("gate_rwkv")) * br_rwkv + _sigmoid(proj("gate_att")) * br_att
    resid = x + _mm(merged, w_out_ref[...])
    out_ref[0] = _rms_norm(resid, g_final_ref[...])


def _const_spec(shape):
    return pl.BlockSpec(shape, lambda b, t: (0,) * len(shape), pipeline_mode=pl.Buffered(1))


def kernel(x, g_pre, w_in, mu_shift, w0, w_decay_up, a0, w_iclr_up, k_k, k_a, r_k, gn_w, gn_b, b_qkv,
           sinks, w_branch_rwkv, w_branch_att, w_out, g_final):
    assert g_pre.shape[0] == 1, "single layer"
    batch, seq, d = x.shape
    assert d == D_MODEL and seq % TOKENS_PER_STEP == 0 and w_in.shape[-1] == IN_COLS
    row = lambda p: p.reshape(1, -1).astype(F32)
    w_lora = jnp.zeros((2 * LORA, 2 * RWKV_WIDTH), F32)
    w_lora = w_lora.at[:LORA, :RWKV_WIDTH].set(w_decay_up[0]).at[LORA:, RWKV_WIDTH:].set(w_iclr_up[0])
    lane_head = jnp.arange(MXU_TILE) // HEAD
    head_ones = (lane_head[:, None] == lane_head[None, :]).astype(BF16)
    tri = (jnp.arange(CHUNK)[:, None] >= jnp.arange(CHUNK)[None, :]).astype(BF16)
    operands = [
        x, row(g_pre[0]), w_in[0], row(mu_shift[0]),
        row(w0[0]), w_lora.astype(BF16), row(a0[0]),
        row(k_k[0]), row(k_a[0]), row(r_k[0]), row(gn_w[0]), row(gn_b[0]), row(b_qkv[0]),
        w_branch_rwkv[0].astype(BF16), w_branch_att[0].astype(BF16), w_out[0].astype(BF16), row(g_final),
        head_ones, tri,
    ]
    x_spec = pl.BlockSpec((1, TOKENS_PER_STEP, D_MODEL), lambda b, t: (b, t, 0))
    in_specs = [pl.BlockSpec(memory_space=pltpu.SMEM), x_spec] + [_const_spec(op.shape) for op in operands[1:]]
    in_specs[1 + 2] = pl.BlockSpec(memory_space=pl.ANY)
    return pl.pallas_call(
        _layer_kernel,
        grid=(batch, seq // TOKENS_PER_STEP),
        in_specs=in_specs,
        out_specs=x_spec,
        out_shape=jax.ShapeDtypeStruct(x.shape, x.dtype),
        scratch_shapes=[
            pltpu.VMEM((D_MODEL, IN_COLS), BF16),
            pltpu.VMEM((2, W_IN_COPY_ROWS, IN_COLS), F32),
            pltpu.SemaphoreType.DMA((2,)),
            pltpu.VMEM((RWKV_WIDTH // SLAB, SLAB, SLAB), F32),
            pltpu.VMEM((8, SHIFT_WIDTH), F32),
            pltpu.VMEM((ATT_BLOCK, ATT_KV_WIDTH), F32),
            pltpu.VMEM((ATT_BLOCK, ATT_KV_WIDTH), F32),
        ],
        compiler_params=pltpu.CompilerParams(
            dimension_semantics=("arbitrary", "arbitrary"),
            vmem_limit_bytes=VMEM_LIMIT_BYTES,
        ),
        name="hybrid_rwkv7_swa_layer",
    )(sinks[0].astype(F32), *operands)
```

```python
import functools
import math

import jax
import jax.numpy as jnp
from jax import lax
from jax.experimental import pallas as pl
from jax.experimental.pallas import tpu as pltpu

D_MODEL = 1024
HEAD = 64
HEAD_SHIFT = 6
RWKV_WIDTH = 512
LORA = 64
ATT_WIDTH = 512
ATT_KV_WIDTH = 128
ATT_GROUP = 4
ATT_BLOCK = 128
QKV_WIDTH = ATT_WIDTH + 2 * ATT_KV_WIDTH
RKV_WIDTH = 3 * RWKV_WIDTH
SHIFT_WIDTH = RKV_WIDTH + 2 * LORA
PROJ_COLS = {"r": (0, RWKV_WIDTH), "k": (RWKV_WIDTH, 2 * RWKV_WIDTH), "v": (2 * RWKV_WIDTH, RKV_WIDTH),
             "lo": (RKV_WIDTH, SHIFT_WIDTH)}
_off = SHIFT_WIDTH
for _name, _width in (("g_rwkv", RWKV_WIDTH), ("qkv", QKV_WIDTH), ("g_att", ATT_WIDTH),
                      ("gate_rwkv", D_MODEL), ("gate_att", D_MODEL)):
    PROJ_COLS[_name] = (_off, _off + _width)
    _off += _width
IN_COLS = _off
RMS_EPS = 1e-6
GN_EPS = 64e-5
L2_EPS = 1e-12
NEG_INF = -1e30

LANES = 128
MXU_TILE = 256
SLAB = 128
TOKENS_PER_STEP = 512
CHUNK = 64
SQUARINGS = int(math.log2(CHUNK))
VMEM_LIMIT_BYTES = 56 * 1024 * 1024
WEIGHT_COPY_ROWS = 128

F32 = jnp.float32
BF16 = jnp.bfloat16


def _mm(a, b, *, tb=False, passes=1):
    dn = (((1,), (1 if tb else 0,)), ((), ()))
    dot = lambda x, y: lax.dot_general(x, y, dn, preferred_element_type=F32)
    a_hi = a.astype(BF16)
    b_hi = b.astype(BF16)
    out = dot(a_hi, b_hi)
    if passes >= 2:
        a_lo = (a - a_hi.astype(F32)).astype(BF16)
        out = out + dot(a_lo, b_hi)
    if passes >= 3:
        b_lo = (b - b_hi.astype(F32)).astype(BF16)
        out = out + dot(a_hi, b_lo)
    return out


def _sigmoid(x):
    return 1.0 / (1.0 + jnp.exp(-x))


def _rms_norm(x, g):
    return x * lax.rsqrt(jnp.mean(x * x, axis=-1, keepdims=True) + RMS_EPS) * g


def _iota(shape, dim):
    return lax.broadcasted_iota(jnp.int32, shape, dim)


def _block_diag(z, mask):
    return jnp.where(mask, jnp.concatenate([z] * (SLAB // CHUNK), axis=0), 0.0)


class _Deferred:
    def __init__(self):
        self._queue = []

    def add(self, thunk, cost):
        self._queue.append((thunk, cost))

    def run(self, budget):
        while self._queue and budget > 0:
            thunk, cost = self._queue.pop(0)
            thunk()
            budget -= cost

    def drain(self):
        self.run(float("inf"))


def _rwkv_chunks(slabs, ptots, n_groups, state_ref, fill_wave, fill_chain):
    heads = SLAB // HEAD
    row = _iota((CHUNK, SLAB), 0)
    col = _iota((CHUNK, SLAB), 1)
    src = col & (HEAD - 1)
    strict = src < row
    incl = src <= row
    eye = src == row
    lane_head = col >> HEAD_SHIFT
    r2 = _iota((SLAB, SLAB), 0)
    c2 = _iota((SLAB, SLAB), 1)
    bd_mask = (r2 >> HEAD_SHIFT) == (c2 >> HEAD_SHIFT)
    bd = functools.partial(_block_diag, mask=bd_mask)
    cat = jnp.concatenate
    n = len(slabs)

    a_all = []
    for rt, bt, at, kt, _, _, _ in slabs:
        rhs_t = cat([jnp.where(lane_head == hd, z, 0.0) for z in (at, kt) for hd in range(heads)], axis=0)
        a_all.append(_mm(cat([bt, rt], axis=0), rhs_t, tb=True))
    a_ba = [jnp.where(strict, a[:CHUNK, :SLAB], 0.0) for a in a_all]
    a_bk = [jnp.where(strict, a[:CHUNK, SLAB:], 0.0) for a in a_all]
    a_ra = [jnp.where(incl, a[CHUNK:, :SLAB], 0.0) for a in a_all]
    a_rk = [jnp.where(incl, a[CHUNK:, SLAB:], 0.0) for a in a_all]
    fill_wave()

    p = [_mm(a, bd(a)) for a in a_ba]
    av = [_mm(cat([a_bk[i], a_rk[i]], axis=0), bd(slabs[i][6])) for i in range(n)]
    fill_wave()

    x = [jnp.where(eye, 1.0, 0.0) + a for a in a_ba]
    for level in range(2, SQUARINGS + 1):
        if level < SQUARINGS:
            prod = [_mm(cat([x[i], p[i]], axis=0), bd(p[i])) for i in range(n)]
            x = [x[i] + prod[i][:CHUNK] for i in range(n)]
            p = [prod[i][CHUNK:] for i in range(n)]
        else:
            x = [x[i] + _mm(x[i], bd(p[i])) for i in range(n)]
        fill_wave()

    gw = [_mm(x[i], cat([bd(slabs[i][1]), bd(av[i][:CHUNK])], axis=1)) for i in range(n)]
    fill_wave()
    rg = [_mm(a_ra[i], cat([bd(gw[i][:, :SLAB]), bd(gw[i][:, SLAB:])], axis=1)) for i in range(n)]
    r_hat = [slabs[i][0] + rg[i][:, :SLAB] for i in range(n)]
    y0 = [rg[i][:, SLAB:] + av[i][CHUNK:] for i in range(n)]

    tiles = SLAB // LANES
    mn = []
    for i in range(n):
        _, _, _, _, ah, kh, v = slabs[i]
        for tl in range(tiles):
            ln = slice(tl * LANES, (tl + 1) * LANES)
            z_t = cat([ah[:, ln], kh[:, ln]], axis=0).T
            rhs = cat([cat([gw[i][:, ln], gw[i][:, SLAB + tl * LANES:SLAB + (tl + 1) * LANES]], axis=1),
                       cat([jnp.zeros_like(v[:, ln]), v[:, ln]], axis=1)], axis=0)
            mn.append(_mm(z_t, rhs))
    fill_wave()
    zero = jnp.zeros((LANES, LANES), F32)
    tile_diag = lambda blocks: cat([cat([blk if j == tl else zero for j in range(tiles)], axis=1)
                                    for tl, blk in enumerate(blocks)], axis=0)
    m_bd, n_bd = [], []
    for i in range(n):
        parts = mn[i * tiles:(i + 1) * tiles]
        m_full = tile_diag([q[:, :LANES] for q in parts])
        n_full = tile_diag([q[:, LANES:] for q in parts])
        m_bd.append(jnp.where(bd_mask, m_full, 0.0) + jnp.where(r2 == c2, ptots[i], 0.0))
        n_bd.append(jnp.where(bd_mask, n_full, 0.0))

    state = [state_ref[gr] for gr in range(n_groups)]
    ys = []
    for i in range(n):
        gr = i % n_groups
        out = _mm(cat([r_hat[i], m_bd[i]], axis=0), state[gr])
        ys.append(out[:CHUNK] + y0[i])
        state[gr] = out[CHUNK:] + n_bd[i]
        if gr == n_groups - 1:
            fill_chain()
    for gr in range(n_groups):
        state_ref[gr] = state[gr]
    return ys


def _convert_weights(jobs, stage_ref, stage_sem):
    chunks = []
    for src, dst, (row0, col0) in jobs:
        rows, cols = src.shape
        for r in range(0, rows, WEIGHT_COPY_ROWS):
            chunks.append((src, dst, r, min(WEIGHT_COPY_ROWS, rows - r), cols, row0, col0))

    def copy(i):
        src, _, r, n, cols, _, _ = chunks[i]
        window = stage_ref.at[i % 2, pl.ds(0, n), pl.ds(0, cols)]
        return pltpu.make_async_copy(src.at[pl.ds(r, n), :], window, stage_sem.at[i % 2])

    copy(0).start()
    copy(1).start()
    for i, (_, dst, r, n, cols, row0, col0) in enumerate(chunks):
        copy(i).wait()
        dst[row0 + r:row0 + r + n, col0:col0 + cols] = stage_ref[i % 2, :n, :cols].astype(BF16)
        if i + 2 < len(chunks):
            copy(i + 2).start()


def _layer_kernel(sinks_ref, x_ref, g_pre_ref, w_in_hbm_ref, mu_ref, w0_ref, w_decay_up_hbm_ref, w_iclr_up_hbm_ref,
                  a0_ref, k_k_ref, k_a_ref, r_k_ref, gn_w_ref, gn_b_ref, b_qkv_ref, w_br_rwkv_hbm_ref,
                  w_br_att_hbm_ref, w_out_hbm_ref, g_final_ref, head_ones_ref, tri_ref, out_ref,
                  w_in_ref, w_lora_ref, w_br_rwkv_ref, w_br_att_ref, w_out_ref, stage_ref, stage_sem,
                  state_ref, tail_ref, k_prev_ref, v_prev_ref):
    tb = TOKENS_PER_STEP
    t_blk = pl.program_id(1)

    @pl.when((pl.program_id(0) == 0) & (t_blk == 0))
    def _():
        w_lora_ref[...] = jnp.zeros_like(w_lora_ref)
        _convert_weights([(w_in_hbm_ref, w_in_ref, (0, 0)),
                          (w_decay_up_hbm_ref, w_lora_ref, (0, 0)),
                          (w_iclr_up_hbm_ref, w_lora_ref, (LORA, RWKV_WIDTH)),
                          (w_br_rwkv_hbm_ref, w_br_rwkv_ref, (0, 0)),
                          (w_br_att_hbm_ref, w_br_att_ref, (0, 0)),
                          (w_out_hbm_ref, w_out_ref, (0, 0))], stage_ref, stage_sem)

    @pl.when(t_blk == 0)
    def _():
        state_ref[...] = jnp.zeros_like(state_ref)
        tail_ref[...] = jnp.zeros_like(tail_ref)
        k_prev_ref[...] = jnp.zeros_like(k_prev_ref)
        v_prev_ref[...] = jnp.zeros_like(v_prev_ref)

    cat = jnp.concatenate
    x = x_ref[0]
    h = _rms_norm(x, g_pre_ref[...]).astype(BF16)

    tiles = {}

    def project_tile(j):
        tiles[j] = jnp.dot(h, w_in_ref[:, j * MXU_TILE:min((j + 1) * MXU_TILE, IN_COLS)], preferred_element_type=F32)

    def proj(name):
        lo, hi = PROJ_COLS[name]
        return cat([tiles[c // MXU_TILE][:, c % MXU_TILE:c % MXU_TILE + LANES] for c in range(lo, hi, LANES)], axis=1)

    def tiles_of(*names):
        return sorted({c // MXU_TILE for nm in names for c in range(*PROJ_COLS[nm], LANES)})

    def project(*names):
        for j in tiles_of(*names):
            if j not in tiles:
                project_tile(j)

    def token_shift(name):
        p = proj(name)
        cols = slice(PROJ_COLS[name][0], PROJ_COLS[name][1])
        prev = pltpu.roll(p, 1, 0)
        prev = jnp.where(_iota(p.shape, 0) == 0, tail_ref[7:8, cols], prev)
        tail_ref[:, cols] = p[tb - 8:, :]
        return p + (prev - p) * mu_ref[:, cols]

    head_ones = head_ones_ref[...]
    head_sum = lambda z: cat([_mm(z[:, j:j + MXU_TILE], head_ones) for j in range(0, RWKV_WIDTH, MXU_TILE)], axis=1)

    project("lo")
    project("r", "k")
    lo = token_shift("lo")
    lo = jnp.where(_iota(lo.shape, 1) < LORA, jnp.tanh(lo), lo)
    lora = _mm(lo, w_lora_ref[...])
    project("v")
    r = token_shift("r")
    k = token_shift("k")
    kk = k * k_k_ref[...]
    kk_norm2 = head_sum(kk * kk)
    log_w = -math.exp(-0.5) * _sigmoid(w0_ref[...] + lora[:, :RWKV_WIDTH])
    a = _sigmoid(a0_ref[...] + lora[:, RWKV_WIDTH:])
    n_groups = RWKV_WIDTH // SLAB
    n_chunks = tb // CHUNK
    tri = tri_ref[...]
    cums = []
    for c in range(n_chunks):
        lw = log_w[c * CHUNK:(c + 1) * CHUNK]
        lw_hi = lw.astype(BF16)
        lw_mid = (lw - lw_hi.astype(F32)).astype(BF16)
        lw_lo = (lw - lw_hi.astype(F32) - lw_mid.astype(F32)).astype(BF16)
        cums.append(jnp.dot(tri, lw_hi, preferred_element_type=F32)
                    + jnp.dot(tri, lw_mid, preferred_element_type=F32)
                    + jnp.dot(tri, lw_lo, preferred_element_type=F32))
    project("qkv")

    tile_steps = tb // 16
    deferred = _Deferred()
    for j in range(-(-IN_COLS // MXU_TILE)):
        if j not in tiles:
            deferred.add(functools.partial(project_tile, j), tile_steps)
    deferred.run(tile_steps)
    qkv = proj("qkv") + b_qkv_ref[...]

    q = qkv[:, :ATT_WIDTH] * (HEAD ** -0.5)
    k_att = qkv[:, ATT_WIDTH:ATT_WIDTH + ATT_KV_WIDTH]
    v_att = qkv[:, ATT_WIDTH + ATT_KV_WIDTH:]
    qi = _iota((ATT_BLOCK, 2 * ATT_BLOCK), 0)
    kj = _iota((ATT_BLOCK, 2 * ATT_BLOCK), 1)
    band = (kj > qi) & (kj <= qi + ATT_BLOCK)
    kv_lane0 = _iota((2 * ATT_BLOCK, LANES), 1) < HEAD
    out_lane0 = _iota((ATT_BLOCK, LANES), 1) < HEAD
    n_att_heads = ATT_WIDTH // HEAD
    n_att_blocks = tb // ATT_BLOCK
    k_prev = k_prev_ref[...]
    v_prev = v_prev_ref[...]
    scores, v_ops = [], []
    for blk in range(n_att_blocks):
        rows = slice(blk * ATT_BLOCK, (blk + 1) * ATT_BLOCK)
        k_cat = cat([k_prev, k_att[rows]], axis=0)
        v_cat = cat([v_prev, v_att[rows]], axis=0)
        k_swap = pltpu.roll(k_cat, HEAD, 1)
        v_swap = pltpu.roll(v_cat, HEAD, 1)
        k_ops, v_blk = {}, {}
        for kv_head in range(ATT_KV_WIDTH // HEAD):
            for parity in range(2):
                own_lanes = kv_lane0 if parity == 0 else ~kv_lane0
                k_ops[kv_head, parity] = jnp.where(own_lanes, k_cat if kv_head == parity else k_swap, 0.0)
                v_blk[kv_head, parity] = jnp.where(own_lanes, v_cat if kv_head == parity else v_swap, 1.0)
        for hd in range(n_att_heads):
            kv_head, parity, pair = hd // ATT_GROUP, hd % 2, hd // 2
            scores.append(_mm(q[rows, pair * LANES:(pair + 1) * LANES], k_ops[kv_head, parity], tb=True))
            v_ops.append(v_blk[kv_head, parity])
        k_prev, v_prev = k_att[rows], v_att[rows]
    k_prev_ref[...] = k_prev
    v_prev_ref[...] = v_prev

    deferred.run(tile_steps)

    small = _Deferred()
    outs, sink_terms = {}, {}

    def softmax_pv(i):
        blk, hd = divmod(i, n_att_heads)
        valid = band & ((kj >= ATT_BLOCK) | (t_blk > 0)) if blk == 0 else band
        s = jnp.where(valid, scores[i], NEG_INF)
        sink = sinks_ref[hd]
        m = jnp.maximum(jnp.max(s, axis=-1, keepdims=True), sink)
        outs[i] = _mm(jnp.exp(s - m), v_ops[i])
        sink_terms[i] = jnp.exp(sink - m)

    for i in range(len(scores)):
        small.add(functools.partial(softmax_pv, i), ATT_BLOCK // 16)

    v = token_shift("v")
    k = k * (1.0 + (a - 1.0) * k_a_ref[...])
    bonus = head_sum(r * k * r_k_ref[...])
    kk = kk * jnp.minimum(lax.rsqrt(kk_norm2), 1.0 / L2_EPS)
    alpha = -(a * kk)
    deferred.run(2 * tile_steps)
    slabs, ptots = [], []
    for c in range(n_chunks):
        rows = slice(c * CHUNK, (c + 1) * CHUNK)
        cum = cums[c]
        total = cum[CHUNK - 1:CHUNK, :]
        p_incl = jnp.exp(cum)
        p_inv = jnp.exp(-cum)
        p_tail = jnp.exp(total - cum)
        full = (r[rows] * p_incl, kk[rows] * jnp.exp(cum - log_w[rows]), alpha[rows] * p_inv, k[rows] * p_inv,
                alpha[rows] * p_tail, k[rows] * p_tail, v[rows])
        ptot = jnp.exp(total)
        for gr in range(n_groups):
            ln = slice(gr * SLAB, (gr + 1) * SLAB)
            slabs.append(tuple(z[:, ln] for z in full))
            ptots.append(ptot[:, ln])
    y_slabs = _rwkv_chunks(slabs, ptots, n_groups, state_ref,
                           functools.partial(small.run, len(scores) // 8 * (ATT_BLOCK // 16)),
                           functools.partial(deferred.run, tile_steps))
    y = cat([cat(y_slabs[c * n_groups:(c + 1) * n_groups], axis=1) for c in range(n_chunks)], axis=0)
    small.drain()
    deferred.drain()

    mean = head_sum(y) * (1.0 / HEAD)
    yc = y - mean

    att_blocks = []
    for blk in range(n_att_blocks):
        pair_slabs = []
        for pair in range(ATT_WIDTH // LANES):
            i0 = blk * n_att_heads + 2 * pair
            o0, o1 = outs[i0], outs[i0 + 1]
            num = jnp.where(out_lane0, o0, o1)
            den = pltpu.roll(jnp.where(out_lane0, o1, o0), HEAD, 1)
            den = den + jnp.where(out_lane0, sink_terms[i0], sink_terms[i0 + 1])
            pair_slabs.append(num / den)
        att_blocks.append(cat(pair_slabs, axis=1))
    y_att = cat(att_blocks, axis=0)

    g_att = proj("g_att")
    br_att = _mm(y_att * (g_att * _sigmoid(g_att)), w_br_att_ref[...])
    var = head_sum(yc * yc) * (1.0 / HEAD)
    y = yc * lax.rsqrt(var + GN_EPS) * gn_w_ref[...] + gn_b_ref[...]
    y_rwkv = y + bonus * v
    g_rwkv = proj("g_rwkv")
    br_rwkv = _mm(y_rwkv * (g_rwkv * _sigmoid(g_rwkv)), w_br_rwkv_ref[...])
    merged = _sigmoid(proj("gate_rwkv")) * br_rwkv + _sigmoid(proj("gate_att")) * br_att
    resid = x + _mm(merged, w_out_ref[...])
    out_ref[0] = _rms_norm(resid, g_final_ref[...])


def _const_spec(shape):
    return pl.BlockSpec(shape, lambda b, t: (0,) * len(shape), pipeline_mode=pl.Buffered(1))


def kernel(x, g_pre, w_in, mu_shift, w0, w_decay_up, a0, w_iclr_up, k_k, k_a, r_k, gn_w, gn_b, b_qkv,
           sinks, w_branch_rwkv, w_branch_att, w_out, g_final):
    assert g_pre.shape[0] == 1, "single layer"
    batch, seq, d = x.shape
    assert d == D_MODEL and seq % TOKENS_PER_STEP == 0 and w_in.shape[-1] == IN_COLS
    row = lambda p: p.reshape(1, -1).astype(F32)
    lane_head = jnp.arange(MXU_TILE) // HEAD
    head_ones = (lane_head[:, None] == lane_head[None, :]).astype(BF16)
    tri = (jnp.arange(CHUNK)[:, None] >= jnp.arange(CHUNK)[None, :]).astype(BF16)
    x_spec = pl.BlockSpec((1, TOKENS_PER_STEP, D_MODEL), lambda b, t: (b, t, 0))
    in_hbm = pl.BlockSpec(memory_space=pl.ANY)
    operands = [
        (sinks[0].astype(F32), pl.BlockSpec(memory_space=pltpu.SMEM)), (x, x_spec), (row(g_pre[0]), None),
        (w_in[0], in_hbm), (row(mu_shift[0]), None), (row(w0[0]), None), (w_decay_up[0], in_hbm),
        (w_iclr_up[0], in_hbm), (row(a0[0]), None), (row(k_k[0]), None), (row(k_a[0]), None), (row(r_k[0]), None),
        (row(gn_w[0]), None), (row(gn_b[0]), None), (row(b_qkv[0]), None), (w_branch_rwkv[0], in_hbm),
        (w_branch_att[0], in_hbm), (w_out[0], in_hbm), (row(g_final), None), (head_ones, None), (tri, None),
    ]
    in_specs = [spec if spec is not None else _const_spec(op.shape) for op, spec in operands]
    return pl.pallas_call(
        _layer_kernel,
        grid=(batch, seq // TOKENS_PER_STEP),
        in_specs=in_specs,
        out_specs=x_spec,
        out_shape=jax.ShapeDtypeStruct(x.shape, x.dtype),
        scratch_shapes=[
            pltpu.VMEM((D_MODEL, IN_COLS), BF16),
            pltpu.VMEM((2 * LORA, 2 * RWKV_WIDTH), BF16),
            pltpu.VMEM((RWKV_WIDTH, D_MODEL), BF16),
            pltpu.VMEM((ATT_WIDTH, D_MODEL), BF16),
            pltpu.VMEM((D_MODEL, D_MODEL), BF16),
            pltpu.VMEM((2, WEIGHT_COPY_ROWS, IN_COLS), F32),
            pltpu.SemaphoreType.DMA((2,)),
            pltpu.VMEM((RWKV_WIDTH // SLAB, SLAB, SLAB), F32),
            pltpu.VMEM((8, SHIFT_WIDTH), F32),
            pltpu.VMEM((ATT_BLOCK, ATT_KV_WIDTH), F32),
            pltpu.VMEM((ATT_BLOCK, ATT_KV_WIDTH), F32),
        ],
        compiler_params=pltpu.CompilerParams(
            dimension_semantics=("arbitrary", "arbitrary"),
            vmem_limit_bytes=VMEM_LIMIT_BYTES,
        ),
        name="hybrid_rwkv7_swa_layer",
    )(*[op for op, _ in operands])
```

```python
import functools
import math

import jax
import jax.numpy as jnp
from jax import lax
from jax.experimental import pallas as pl
from jax.experimental.pallas import tpu as pltpu

D_MODEL = 1024
HEAD = 64
HEAD_SHIFT = 6
RWKV_WIDTH = 512
LORA = 64
ATT_WIDTH = 512
ATT_KV_WIDTH = 128
ATT_GROUP = 4
ATT_BLOCK = 128
QKV_WIDTH = ATT_WIDTH + 2 * ATT_KV_WIDTH
RKV_WIDTH = 3 * RWKV_WIDTH
SHIFT_WIDTH = RKV_WIDTH + 2 * LORA
PROJ_COLS = {"r": (0, RWKV_WIDTH), "k": (RWKV_WIDTH, 2 * RWKV_WIDTH), "v": (2 * RWKV_WIDTH, RKV_WIDTH),
             "lo": (RKV_WIDTH, SHIFT_WIDTH)}
_off = SHIFT_WIDTH
for _name, _width in (("g_rwkv", RWKV_WIDTH), ("qkv", QKV_WIDTH), ("g_att", ATT_WIDTH),
                      ("gate_rwkv", D_MODEL), ("gate_att", D_MODEL)):
    PROJ_COLS[_name] = (_off, _off + _width)
    _off += _width
IN_COLS = _off
RMS_EPS = 1e-6
GN_EPS = 64e-5
L2_EPS = 1e-12
NEG_INF = -1e30

LANES = 128
MXU_TILE = 256
SLAB = 128
TOKENS_PER_STEP = 512
CHUNK = 64
SQUARINGS = int(math.log2(CHUNK))
VMEM_LIMIT_BYTES = 56 * 1024 * 1024
W_IN_COPY_ROWS = 128

F32 = jnp.float32
BF16 = jnp.bfloat16


def _mm(a, b, *, tb=False, passes=1):
    dn = (((1,), (1 if tb else 0,)), ((), ()))
    dot = lambda x, y: lax.dot_general(x, y, dn, preferred_element_type=F32)
    a_hi = a.astype(BF16)
    b_hi = b.astype(BF16)
    out = dot(a_hi, b_hi)
    if passes >= 2:
        a_lo = (a - a_hi.astype(F32)).astype(BF16)
        out = out + dot(a_lo, b_hi)
    if passes >= 3:
        b_lo = (b - b_hi.astype(F32)).astype(BF16)
        out = out + dot(a_hi, b_lo)
    return out


def _sigmoid(x):
    return 1.0 / (1.0 + jnp.exp(-x))


def _rms_norm(x, g):
    return x * lax.rsqrt(jnp.mean(x * x, axis=-1, keepdims=True) + RMS_EPS) * g


def _iota(shape, dim):
    return lax.broadcasted_iota(jnp.int32, shape, dim)


def _block_diag(z, mask):
    return jnp.where(mask, jnp.concatenate([z] * (SLAB // CHUNK), axis=0), 0.0)


class _Deferred:
    def __init__(self):
        self._queue = []

    def add(self, thunk, cost):
        self._queue.append((thunk, cost))

    def run(self, budget):
        while self._queue and budget > 0:
            thunk, cost = self._queue.pop(0)
            thunk()
            budget -= cost

    def drain(self):
        self.run(float("inf"))


def _rwkv_chunks(slabs, ptots, n_groups, state_ref, fill_wave, fill_chain):
    heads = SLAB // HEAD
    row = _iota((CHUNK, SLAB), 0)
    col = _iota((CHUNK, SLAB), 1)
    src = col & (HEAD - 1)
    strict = src < row
    incl = src <= row
    eye = src == row
    lane_head = col >> HEAD_SHIFT
    r2 = _iota((SLAB, SLAB), 0)
    c2 = _iota((SLAB, SLAB), 1)
    bd_mask = (r2 >> HEAD_SHIFT) == (c2 >> HEAD_SHIFT)
    bd = functools.partial(_block_diag, mask=bd_mask)
    cat = jnp.concatenate
    n = len(slabs)

    a_all = []
    for rt, bt, at, kt, _, _, _ in slabs:
        rhs_t = cat([jnp.where(lane_head == hd, z, 0.0) for z in (at, kt) for hd in range(heads)], axis=0)
        a_all.append(_mm(cat([bt, rt], axis=0), rhs_t, tb=True))
    a_ba = [jnp.where(strict, a[:CHUNK, :SLAB], 0.0) for a in a_all]
    a_bk = [jnp.where(strict, a[:CHUNK, SLAB:], 0.0) for a in a_all]
    a_ra = [jnp.where(incl, a[CHUNK:, :SLAB], 0.0) for a in a_all]
    a_rk = [jnp.where(incl, a[CHUNK:, SLAB:], 0.0) for a in a_all]
    fill_wave()

    p = [_mm(a, bd(a)) for a in a_ba]
    av = [_mm(cat([a_bk[i], a_rk[i]], axis=0), bd(slabs[i][6])) for i in range(n)]
    fill_wave()

    x = [jnp.where(eye, 1.0, 0.0) + a for a in a_ba]
    for level in range(2, SQUARINGS + 1):
        if level < SQUARINGS:
            prod = [_mm(cat([x[i], p[i]], axis=0), bd(p[i])) for i in range(n)]
            x = [x[i] + prod[i][:CHUNK] for i in range(n)]
            p = [prod[i][CHUNK:] for i in range(n)]
        else:
            x = [x[i] + _mm(x[i], bd(p[i])) for i in range(n)]
        fill_wave()

    gw = [_mm(x[i], cat([bd(slabs[i][1]), bd(av[i][:CHUNK])], axis=1)) for i in range(n)]
    fill_wave()
    rg = [_mm(a_ra[i], cat([bd(gw[i][:, :SLAB]), bd(gw[i][:, SLAB:])], axis=1)) for i in range(n)]
    r_hat = [slabs[i][0] + rg[i][:, :SLAB] for i in range(n)]
    y0 = [rg[i][:, SLAB:] + av[i][CHUNK:] for i in range(n)]

    tiles = SLAB // LANES
    mn = []
    for i in range(n):
        _, _, _, _, ah, kh, v = slabs[i]
        for tl in range(tiles):
            ln = slice(tl * LANES, (tl + 1) * LANES)
            z_t = cat([ah[:, ln], kh[:, ln]], axis=0).T
            rhs = cat([cat([gw[i][:, ln], gw[i][:, SLAB + tl * LANES:SLAB + (tl + 1) * LANES]], axis=1),
                       cat([jnp.zeros_like(v[:, ln]), v[:, ln]], axis=1)], axis=0)
            mn.append(_mm(z_t, rhs))
    fill_wave()
    zero = jnp.zeros((LANES, LANES), F32)
    tile_diag = lambda blocks: cat([cat([blk if j == tl else zero for j in range(tiles)], axis=1)
                                    for tl, blk in enumerate(blocks)], axis=0)
    m_bd, n_bd = [], []
    for i in range(n):
        parts = mn[i * tiles:(i + 1) * tiles]
        m_full = tile_diag([q[:, :LANES] for q in parts])
        n_full = tile_diag([q[:, LANES:] for q in parts])
        m_bd.append(jnp.where(bd_mask, m_full, 0.0) + jnp.where(r2 == c2, ptots[i], 0.0))
        n_bd.append(jnp.where(bd_mask, n_full, 0.0))

    state = [state_ref[gr] for gr in range(n_groups)]
    ys = []
    for i in range(n):
        gr = i % n_groups
        out = _mm(cat([r_hat[i], m_bd[i]], axis=0), state[gr])
        ys.append(out[:CHUNK] + y0[i])
        state[gr] = out[CHUNK:] + n_bd[i]
        if gr == n_groups - 1:
            fill_chain()
    for gr in range(n_groups):
        state_ref[gr] = state[gr]
    return ys


def _load_w_in(w_in_hbm_ref, w_in_ref, stage_ref, stage_sem):
    n_copies = D_MODEL // W_IN_COPY_ROWS

    def copy(i):
        rows = pl.ds(i * W_IN_COPY_ROWS, W_IN_COPY_ROWS)
        return pltpu.make_async_copy(w_in_hbm_ref.at[rows, :], stage_ref.at[i % 2], stage_sem.at[i % 2])

    copy(0).start()
    copy(1).start()
    for i in range(n_copies):
        copy(i).wait()
        w_in_ref[i * W_IN_COPY_ROWS:(i + 1) * W_IN_COPY_ROWS, :] = stage_ref[i % 2].astype(BF16)
        if i + 2 < n_copies:
            copy(i + 2).start()


def _layer_kernel(sinks_ref, x_ref, g_pre_ref, w_in_hbm_ref, mu_ref, w0_ref, w_decay_up_ref, w_iclr_up_ref, a0_ref,
                  k_k_ref, k_a_ref, r_k_ref, gn_w_ref, gn_b_ref, b_qkv_ref, w_br_rwkv_ref, w_br_att_ref,
                  w_out_ref, g_final_ref, out_ref,
                  w_in_ref, stage_ref, stage_sem, state_ref, tail_ref, k_prev_ref, v_prev_ref):
    tb = TOKENS_PER_STEP
    t_blk = pl.program_id(1)

    @pl.when((pl.program_id(0) == 0) & (t_blk == 0))
    def _():
        _load_w_in(w_in_hbm_ref, w_in_ref, stage_ref, stage_sem)

    @pl.when(t_blk == 0)
    def _():
        state_ref[...] = jnp.zeros_like(state_ref)
        tail_ref[...] = jnp.zeros_like(tail_ref)
        k_prev_ref[...] = jnp.zeros_like(k_prev_ref)
        v_prev_ref[...] = jnp.zeros_like(v_prev_ref)

    cat = jnp.concatenate
    x = x_ref[0]
    h = _rms_norm(x, g_pre_ref[...]).astype(BF16)

    tiles = {}

    def project_tile(j):
        tiles[j] = jnp.dot(h, w_in_ref[:, j * MXU_TILE:min((j + 1) * MXU_TILE, IN_COLS)], preferred_element_type=F32)

    def proj(name):
        lo, hi = PROJ_COLS[name]
        return cat([tiles[c // MXU_TILE][:, c % MXU_TILE:c % MXU_TILE + LANES] for c in range(lo, hi, LANES)], axis=1)

    def tiles_of(*names):
        return sorted({c // MXU_TILE for nm in names for c in range(*PROJ_COLS[nm], LANES)})

    def project(*names):
        for j in tiles_of(*names):
            if j not in tiles:
                project_tile(j)

    def token_shift(name):
        p = proj(name)
        cols = slice(PROJ_COLS[name][0], PROJ_COLS[name][1])
        prev = pltpu.roll(p, 1, 0)
        prev = jnp.where(_iota(p.shape, 0) == 0, tail_ref[7:8, cols], prev)
        tail_ref[:, cols] = p[tb - 8:, :]
        return p + (prev - p) * mu_ref[:, cols]

    head_of = lambda dim: _iota((MXU_TILE, MXU_TILE), dim) >> HEAD_SHIFT
    head_ones = jnp.where(head_of(0) == head_of(1), 1.0, 0.0).astype(BF16)
    head_sum = lambda z: cat([_mm(z[:, j:j + MXU_TILE], head_ones) for j in range(0, RWKV_WIDTH, MXU_TILE)], axis=1)

    project("lo")
    project("r", "k")
    lo = token_shift("lo")
    lo = jnp.where(_iota(lo.shape, 1) < LORA, jnp.tanh(lo), lo)
    zeros = jnp.zeros((LORA, RWKV_WIDTH), F32)
    w_lora = cat([cat([w_decay_up_ref[...], zeros], axis=1), cat([zeros, w_iclr_up_ref[...]], axis=1)], axis=0)
    lora = _mm(lo, w_lora)
    project("v")
    r = token_shift("r")
    k = token_shift("k")
    kk = k * k_k_ref[...]
    kk_norm2 = head_sum(kk * kk)
    log_w = -math.exp(-0.5) * _sigmoid(w0_ref[...] + lora[:, :RWKV_WIDTH])
    a = _sigmoid(a0_ref[...] + lora[:, RWKV_WIDTH:])
    n_groups = RWKV_WIDTH // SLAB
    n_chunks = tb // CHUNK
    tri = jnp.where(_iota((CHUNK, CHUNK), 0) >= _iota((CHUNK, CHUNK), 1), 1.0, 0.0).astype(BF16)
    cums = []
    for c in range(n_chunks):
        lw = log_w[c * CHUNK:(c + 1) * CHUNK]
        lw_hi = lw.astype(BF16)
        lw_mid = (lw - lw_hi.astype(F32)).astype(BF16)
        lw_lo = (lw - lw_hi.astype(F32) - lw_mid.astype(F32)).astype(BF16)
        cums.append(jnp.dot(tri, lw_hi, preferred_element_type=F32)
                    + jnp.dot(tri, lw_mid, preferred_element_type=F32)
                    + jnp.dot(tri, lw_lo, preferred_element_type=F32))
    project("qkv")

    tile_steps = tb // 16
    deferred = _Deferred()
    for j in range(-(-IN_COLS // MXU_TILE)):
        if j not in tiles:
            deferred.add(functools.partial(project_tile, j), tile_steps)
    deferred.run(tile_steps)
    qkv = proj("qkv") + b_qkv_ref[...]

    q = qkv[:, :ATT_WIDTH] * (HEAD ** -0.5)
    k_att = qkv[:, ATT_WIDTH:ATT_WIDTH + ATT_KV_WIDTH]
    v_att = qkv[:, ATT_WIDTH + ATT_KV_WIDTH:]
    qi = _iota((ATT_BLOCK, 2 * ATT_BLOCK), 0)
    kj = _iota((ATT_BLOCK, 2 * ATT_BLOCK), 1)
    band = (kj > qi) & (kj <= qi + ATT_BLOCK)
    kv_lane0 = _iota((2 * ATT_BLOCK, LANES), 1) < HEAD
    out_lane0 = _iota((ATT_BLOCK, LANES), 1) < HEAD
    n_att_heads = ATT_WIDTH // HEAD
    n_att_blocks = tb // ATT_BLOCK
    k_prev = k_prev_ref[...]
    v_prev = v_prev_ref[...]
    scores, v_ops = [], []
    for blk in range(n_att_blocks):
        rows = slice(blk * ATT_BLOCK, (blk + 1) * ATT_BLOCK)
        k_cat = cat([k_prev, k_att[rows]], axis=0)
        v_cat = cat([v_prev, v_att[rows]], axis=0)
        k_swap = pltpu.roll(k_cat, HEAD, 1)
        v_swap = pltpu.roll(v_cat, HEAD, 1)
        k_ops, v_blk = {}, {}
        for kv_head in range(ATT_KV_WIDTH // HEAD):
            for parity in range(2):
                own_lanes = kv_lane0 if parity == 0 else ~kv_lane0
                k_ops[kv_head, parity] = jnp.where(own_lanes, k_cat if kv_head == parity else k_swap, 0.0)
                v_blk[kv_head, parity] = jnp.where(own_lanes, v_cat if kv_head == parity else v_swap, 1.0)
        for hd in range(n_att_heads):
            kv_head, parity, pair = hd // ATT_GROUP, hd % 2, hd // 2
            scores.append(_mm(q[rows, pair * LANES:(pair + 1) * LANES], k_ops[kv_head, parity], tb=True))
            v_ops.append(v_blk[kv_head, parity])
        k_prev, v_prev = k_att[rows], v_att[rows]
    k_prev_ref[...] = k_prev
    v_prev_ref[...] = v_prev

    deferred.run(tile_steps)

    small = _Deferred()
    outs, sink_terms = {}, {}

    def softmax_pv(i):
        blk, hd = divmod(i, n_att_heads)
        valid = band & ((kj >= ATT_BLOCK) | (t_blk > 0)) if blk == 0 else band
        s = jnp.where(valid, scores[i], NEG_INF)
        sink = sinks_ref[hd]
        m = jnp.maximum(jnp.max(s, axis=-1, keepdims=True), sink)
        outs[i] = _mm(jnp.exp(s - m), v_ops[i])
        sink_terms[i] = jnp.exp(sink - m)

    for i in range(len(scores)):
        small.add(functools.partial(softmax_pv, i), ATT_BLOCK // 16)

    v = token_shift("v")
    k = k * (1.0 + (a - 1.0) * k_a_ref[...])
    bonus = head_sum(r * k * r_k_ref[...])
    kk = kk * jnp.minimum(lax.rsqrt(kk_norm2), 1.0 / L2_EPS)
    alpha = -(a * kk)
    deferred.run(2 * tile_steps)
    slabs, ptots = [], []
    for c in range(n_chunks):
        rows = slice(c * CHUNK, (c + 1) * CHUNK)
        cum = cums[c]
        total = cum[CHUNK - 1:CHUNK, :]
        p_incl = jnp.exp(cum)
        p_inv = jnp.exp(-cum)
        p_tail = jnp.exp(total - cum)
        full = (r[rows] * p_incl, kk[rows] * jnp.exp(cum - log_w[rows]), alpha[rows] * p_inv, k[rows] * p_inv,
                alpha[rows] * p_tail, k[rows] * p_tail, v[rows])
        ptot = jnp.exp(total)
        for gr in range(n_groups):
            ln = slice(gr * SLAB, (gr + 1) * SLAB)
            slabs.append(tuple(z[:, ln] for z in full))
            ptots.append(ptot[:, ln])
    y_slabs = _rwkv_chunks(slabs, ptots, n_groups, state_ref,
                           functools.partial(small.run, len(scores) // 8 * (ATT_BLOCK // 16)),
                           functools.partial(deferred.run, tile_steps))
    y = cat([cat(y_slabs[c * n_groups:(c + 1) * n_groups], axis=1) for c in range(n_chunks)], axis=0)
    small.drain()
    deferred.drain()

    mean = head_sum(y) * (1.0 / HEAD)
    yc = y - mean

    att_blocks = []
    for blk in range(n_att_blocks):
        pair_slabs = []
        for pair in range(ATT_WIDTH // LANES):
            i0 = blk * n_att_heads + 2 * pair
            o0, o1 = outs[i0], outs[i0 + 1]
            num = jnp.where(out_lane0, o0, o1)
            den = pltpu.roll(jnp.where(out_lane0, o1, o0), HEAD, 1)
            den = den + jnp.where(out_lane0, sink_terms[i0], sink_terms[i0 + 1])
            pair_slabs.append(num / den)
        att_blocks.append(cat(pair_slabs, axis=1))
    y_att = cat(att_blocks, axis=0)

    g_att = proj("g_att")
    br_att = _mm(y_att * (g_att * _sigmoid(g_att)), w_br_att_ref[...])
    var = head_sum(yc * yc) * (1.0 / HEAD)
    y = yc * lax.rsqrt(var + GN_EPS) * gn_w_ref[...] + gn_b_ref[...]
    y_rwkv = y + bonus * v
    g_rwkv = proj("g_rwkv")
    br_rwkv = _mm(y_rwkv * (g_rwkv * _sigmoid(g_rwkv)), w_br_rwkv_ref[...])
    merged = _sigmoid(proj("gate_rwkv")) * br_rwkv + _sigmoid(proj("gate_att")) * br_att
    resid = x + _mm(merged, w_out_ref[...])
    out_ref[0] = _rms_norm(resid, g_final_ref[...])


def _const_spec(shape):
    return pl.BlockSpec(shape, lambda b, t: (0,) * len(shape), pipeline_mode=pl.Buffered(1))


def kernel(x, g_pre, w_in, mu_shift, w0, w_decay_up, a0, w_iclr_up, k_k, k_a, r_k, gn_w, gn_b, b_qkv,
           sinks, w_branch_rwkv, w_branch_att, w_out, g_final):
    assert g_pre.shape[0] == 1, "single layer"
    batch, seq, d = x.shape
    assert d == D_MODEL and seq % TOKENS_PER_STEP == 0 and w_in.shape[-1] == IN_COLS
    row = lambda p: p.reshape(1, -1).astype(F32)
    operands = [
        x, row(g_pre[0]), w_in[0], row(mu_shift[0]),
        row(w0[0]), w_decay_up[0], w_iclr_up[0], row(a0[0]),
        row(k_k[0]), row(k_a[0]), row(r_k[0]), row(gn_w[0]), row(gn_b[0]), row(b_qkv[0]),
        w_branch_rwkv[0].astype(BF16), w_branch_att[0].astype(BF16), w_out[0].astype(BF16), row(g_final),
    ]
    x_spec = pl.BlockSpec((1, TOKENS_PER_STEP, D_MODEL), lambda b, t: (b, t, 0))
    in_specs = [pl.BlockSpec(memory_space=pltpu.SMEM), x_spec] + [_const_spec(op.shape) for op in operands[1:]]
    in_specs[1 + 2] = pl.BlockSpec(memory_space=pl.ANY)
    return pl.pallas_call(
        _layer_kernel,
        grid=(batch, seq // TOKENS_PER_STEP),
        in_specs=in_specs,
        out_specs=x_spec,
        out_shape=jax.ShapeDtypeStruct(x.shape, x.dtype),
        scratch_shapes=[
            pltpu.VMEM((D_MODEL, IN_COLS), BF16),
            pltpu.VMEM((2, W_IN_COPY_ROWS, IN_COLS), F32),
            pltpu.SemaphoreType.DMA((2,)),
            pltpu.VMEM((RWKV_WIDTH // SLAB, SLAB, SLAB), F32),
            pltpu.VMEM((8, SHIFT_WIDTH), F32),
            pltpu.VMEM((ATT_BLOCK, ATT_KV_WIDTH), F32),
            pltpu.VMEM((ATT_BLOCK, ATT_KV_WIDTH), F32),
        ],
        compiler_params=pltpu.CompilerParams(
            dimension_semantics=("arbitrary", "arbitrary"),
            vmem_limit_bytes=VMEM_LIMIT_BYTES,
        ),
        name="hybrid_rwkv7_swa_layer",
    )(sinks[0].astype(F32), *operands)
```

```python
import functools
import math

import jax
import jax.numpy as jnp
from jax import lax
from jax.experimental import pallas as pl
from jax.experimental.pallas import tpu as pltpu

D_MODEL = 1024
HEAD = 64
HEAD_SHIFT = 6
RWKV_WIDTH = 512
LORA = 64
ATT_WIDTH = 512
ATT_KV_WIDTH = 128
ATT_GROUP = 4
ATT_BLOCK = 128
QKV_WIDTH = ATT_WIDTH + 2 * ATT_KV_WIDTH
RKV_WIDTH = 3 * RWKV_WIDTH
SHIFT_WIDTH = RKV_WIDTH + 2 * LORA
PROJ_COLS = {"r": (0, RWKV_WIDTH), "k": (RWKV_WIDTH, 2 * RWKV_WIDTH), "v": (2 * RWKV_WIDTH, RKV_WIDTH),
             "lo": (RKV_WIDTH, SHIFT_WIDTH)}
_off = SHIFT_WIDTH
for _name, _width in (("g_rwkv", RWKV_WIDTH), ("qkv", QKV_WIDTH), ("g_att", ATT_WIDTH),
                      ("gate_rwkv", D_MODEL), ("gate_att", D_MODEL)):
    PROJ_COLS[_name] = (_off, _off + _width)
    _off += _width
IN_COLS = _off
RMS_EPS = 1e-6
GN_EPS = 64e-5
L2_EPS = 1e-12
NEG_INF = -1e30

LANES = 128
MXU_TILE = 256
SLAB = 128
TOKENS_PER_STEP = 512
CHUNK = 64
SQUARINGS = int(math.log2(CHUNK))
VMEM_LIMIT_BYTES = 56 * 1024 * 1024
W_IN_COPY_ROWS = 128

F32 = jnp.float32
BF16 = jnp.bfloat16


def _mm(a, b, *, tb=False, passes=1):
    dn = (((1,), (1 if tb else 0,)), ((), ()))
    dot = lambda x, y: lax.dot_general(x, y, dn, preferred_element_type=F32)
    a_hi = a.astype(BF16)
    b_hi = b.astype(BF16)
    out = dot(a_hi, b_hi)
    if passes >= 2:
        a_lo = (a - a_hi.astype(F32)).astype(BF16)
        out = out + dot(a_lo, b_hi)
    if passes >= 3:
        b_lo = (b - b_hi.astype(F32)).astype(BF16)
        out = out + dot(a_hi, b_lo)
    return out


def _sigmoid(x):
    return 1.0 / (1.0 + jnp.exp(-x))


def _rms_norm(x, g):
    return x * lax.rsqrt(jnp.mean(x * x, axis=-1, keepdims=True) + RMS_EPS) * g


def _iota(shape, dim):
    return lax.broadcasted_iota(jnp.int32, shape, dim)


def _block_diag(z, mask):
    return jnp.where(mask, jnp.concatenate([z] * (SLAB // CHUNK), axis=0), 0.0)


class _Deferred:
    def __init__(self):
        self._queue = []

    def add(self, thunk, cost):
        self._queue.append((thunk, cost))

    def run(self, budget):
        while self._queue and budget > 0:
            thunk, cost = self._queue.pop(0)
            thunk()
            budget -= cost

    def drain(self):
        self.run(float("inf"))


def _rwkv_chunks(slabs, ptots, n_groups, state_ref, fill_wave, fill_chain):
    heads = SLAB // HEAD
    row = _iota((CHUNK, SLAB), 0)
    col = _iota((CHUNK, SLAB), 1)
    src = col & (HEAD - 1)
    strict = src < row
    incl = src <= row
    eye = src == row
    lane_head = col >> HEAD_SHIFT
    r2 = _iota((SLAB, SLAB), 0)
    c2 = _iota((SLAB, SLAB), 1)
    bd_mask = (r2 >> HEAD_SHIFT) == (c2 >> HEAD_SHIFT)
    bd = functools.partial(_block_diag, mask=bd_mask)
    cat = jnp.concatenate
    n = len(slabs)

    a_all = []
    for rt, bt, at, kt, _, _, _ in slabs:
        rhs_t = cat([jnp.where(lane_head == hd, z, 0.0) for z in (at, kt) for hd in range(heads)], axis=0)
        a_all.append(_mm(cat([bt, rt], axis=0), rhs_t, tb=True))
    a_ba = [jnp.where(strict, a[:CHUNK, :SLAB], 0.0) for a in a_all]
    a_bk = [jnp.where(strict, a[:CHUNK, SLAB:], 0.0) for a in a_all]
    a_ra = [jnp.where(incl, a[CHUNK:, :SLAB], 0.0) for a in a_all]
    a_rk = [jnp.where(incl, a[CHUNK:, SLAB:], 0.0) for a in a_all]
    fill_wave()

    p = [_mm(a, bd(a)) for a in a_ba]
    av = [_mm(cat([a_bk[i], a_rk[i]], axis=0), bd(slabs[i][6])) for i in range(n)]
    fill_wave()

    x = [jnp.where(eye, 1.0, 0.0) + a for a in a_ba]
    for level in range(2, SQUARINGS + 1):
        if level < SQUARINGS:
            prod = [_mm(cat([x[i], p[i]], axis=0), bd(p[i])) for i in range(n)]
            x = [x[i] + prod[i][:CHUNK] for i in range(n)]
            p = [prod[i][CHUNK:] for i in range(n)]
        else:
            x = [x[i] + _mm(x[i], bd(p[i])) for i in range(n)]
        fill_wave()

    gw = [_mm(x[i], cat([bd(slabs[i][1]), bd(av[i][:CHUNK])], axis=1)) for i in range(n)]
    fill_wave()
    rg = [_mm(a_ra[i], cat([bd(gw[i][:, :SLAB]), bd(gw[i][:, SLAB:])], axis=1)) for i in range(n)]
    r_hat = [slabs[i][0] + rg[i][:, :SLAB] for i in range(n)]
    y0 = [rg[i][:, SLAB:] + av[i][CHUNK:] for i in range(n)]

    tiles = SLAB // LANES
    mn = []
    for i in range(n):
        _, _, _, _, ah, kh, v = slabs[i]
        for tl in range(tiles):
            ln = slice(tl * LANES, (tl + 1) * LANES)
            z_t = cat([ah[:, ln], kh[:, ln]], axis=0).T
            rhs = cat([cat([gw[i][:, ln], gw[i][:, SLAB + tl * LANES:SLAB + (tl + 1) * LANES]], axis=1),
                       cat([jnp.zeros_like(v[:, ln]), v[:, ln]], axis=1)], axis=0)
            mn.append(_mm(z_t, rhs))
    fill_wave()
    zero = jnp.zeros((LANES, LANES), F32)
    tile_diag = lambda blocks: cat([cat([blk if j == tl else zero for j in range(tiles)], axis=1)
                                    for tl, blk in enumerate(blocks)], axis=0)
    m_bd, n_bd = [], []
    for i in range(n):
        parts = mn[i * tiles:(i + 1) * tiles]
        m_full = tile_diag([q[:, :LANES] for q in parts])
        n_full = tile_diag([q[:, LANES:] for q in parts])
        m_bd.append(jnp.where(bd_mask, m_full, 0.0) + jnp.where(r2 == c2, ptots[i], 0.0))
        n_bd.append(jnp.where(bd_mask, n_full, 0.0))

    state = [state_ref[gr] for gr in range(n_groups)]
    ys = []
    for i in range(n):
        gr = i % n_groups
        out = _mm(cat([r_hat[i], m_bd[i]], axis=0), state[gr])
        ys.append(out[:CHUNK] + y0[i])
        state[gr] = out[CHUNK:] + n_bd[i]
        if gr == n_groups - 1:
            fill_chain()
    for gr in range(n_groups):
        state_ref[gr] = state[gr]
    return ys


def _load_w_in(w_in_hbm_ref, w_in_ref, stage_ref, stage_sem):
    n_copies = D_MODEL // W_IN_COPY_ROWS

    def copy(i):
        rows = pl.ds(i * W_IN_COPY_ROWS, W_IN_COPY_ROWS)
        return pltpu.make_async_copy(w_in_hbm_ref.at[rows, :], stage_ref.at[i % 2], stage_sem.at[i % 2])

    copy(0).start()
    copy(1).start()
    for i in range(n_copies):
        copy(i).wait()
        w_in_ref[i * W_IN_COPY_ROWS:(i + 1) * W_IN_COPY_ROWS, :] = stage_ref[i % 2].astype(BF16)
        if i + 2 < n_copies:
            copy(i + 2).start()


def _layer_kernel(sinks_ref, x_ref, g_pre_ref, w_in_hbm_ref, w0_ref, w_decay_up_ref, w_iclr_up_ref, a0_ref,
                  k_k_ref, k_a_ref, r_k_ref, gn_w_ref, gn_b_ref, w_br_rwkv_ref, w_br_att_ref,
                  w_out_ref, g_final_ref, mu_ref, b_qkv_ref, out_ref,
                  tail_ref, w_in_ref, stage_ref, stage_sem, state_ref, k_prev_ref, v_prev_ref):
    tb = TOKENS_PER_STEP
    t_blk = pl.program_id(1)

    @pl.when((pl.program_id(0) == 0) & (t_blk == 0))
    def _():
        _load_w_in(w_in_hbm_ref, w_in_ref, stage_ref, stage_sem)

    @pl.when(t_blk == 0)
    def _():
        state_ref[...] = jnp.zeros_like(state_ref)
        tail_ref[...] = jnp.zeros_like(tail_ref)
        k_prev_ref[...] = jnp.zeros_like(k_prev_ref)
        v_prev_ref[...] = jnp.zeros_like(v_prev_ref)

    cat = jnp.concatenate
    x = x_ref[0]
    h = _rms_norm(x, g_pre_ref[...]).astype(BF16)

    tiles = {}

    def project_tile(j):
        tiles[j] = jnp.dot(h, w_in_ref[:, j * MXU_TILE:min((j + 1) * MXU_TILE, IN_COLS)], preferred_element_type=F32)

    def proj(name):
        lo, hi = PROJ_COLS[name]
        return cat([tiles[c // MXU_TILE][:, c % MXU_TILE:c % MXU_TILE + LANES] for c in range(lo, hi, LANES)], axis=1)

    def tiles_of(*names):
        return sorted({c // MXU_TILE for nm in names for c in range(*PROJ_COLS[nm], LANES)})

    def project(*names):
        for j in tiles_of(*names):
            if j not in tiles:
                project_tile(j)

    def token_shift(name):
        p = proj(name)
        cols = slice(PROJ_COLS[name][0], PROJ_COLS[name][1])
        prev = pltpu.roll(p, 1, 0)
        prev = jnp.where(_iota(p.shape, 0) == 0, tail_ref[7:8, cols], prev)
        tail_ref[:, cols] = p[tb - 8:, :]
        return p + (prev - p) * mu_ref[:, cols]

    head_of = lambda dim: _iota((MXU_TILE, MXU_TILE), dim) >> HEAD_SHIFT
    head_ones = jnp.where(head_of(0) == head_of(1), 1.0, 0.0).astype(BF16)
    head_sum = lambda z: cat([_mm(z[:, j:j + MXU_TILE], head_ones) for j in range(0, RWKV_WIDTH, MXU_TILE)], axis=1)

    project("lo")
    project("r", "k")
    lo = token_shift("lo")
    lo = jnp.where(_iota(lo.shape, 1) < LORA, jnp.tanh(lo), lo)
    zeros = jnp.zeros((LORA, RWKV_WIDTH), F32)
    w_lora = cat([cat([w_decay_up_ref[...], zeros], axis=1), cat([zeros, w_iclr_up_ref[...]], axis=1)], axis=0)
    lora = _mm(lo, w_lora)
    project("v")
    r = token_shift("r")
    k = token_shift("k")
    kk = k * k_k_ref[...]
    kk_norm2 = head_sum(kk * kk)
    log_w = -math.exp(-0.5) * _sigmoid(w0_ref[...] + lora[:, :RWKV_WIDTH])
    a = _sigmoid(a0_ref[...] + lora[:, RWKV_WIDTH:])
    n_groups = RWKV_WIDTH // SLAB
    n_chunks = tb // CHUNK
    tri = jnp.where(_iota((CHUNK, CHUNK), 0) >= _iota((CHUNK, CHUNK), 1), 1.0, 0.0).astype(BF16)
    cums = []
    for c in range(n_chunks):
        lw = log_w[c * CHUNK:(c + 1) * CHUNK]
        lw_hi = lw.astype(BF16)
        lw_mid = (lw - lw_hi.astype(F32)).astype(BF16)
        lw_lo = (lw - lw_hi.astype(F32) - lw_mid.astype(F32)).astype(BF16)
        cums.append(jnp.dot(tri, lw_hi, preferred_element_type=F32)
                    + jnp.dot(tri, lw_mid, preferred_element_type=F32)
                    + jnp.dot(tri, lw_lo, preferred_element_type=F32))
    project("qkv")

    tile_steps = tb // 16
    deferred = _Deferred()
    for j in range(-(-IN_COLS // MXU_TILE)):
        if j not in tiles:
            deferred.add(functools.partial(project_tile, j), tile_steps)
    deferred.run(tile_steps)
    qkv = proj("qkv") + b_qkv_ref[...]

    q = qkv[:, :ATT_WIDTH] * (HEAD ** -0.5)
    k_att = qkv[:, ATT_WIDTH:ATT_WIDTH + ATT_KV_WIDTH]
    v_att = qkv[:, ATT_WIDTH + ATT_KV_WIDTH:]
    qi = _iota((ATT_BLOCK, 2 * ATT_BLOCK), 0)
    kj = _iota((ATT_BLOCK, 2 * ATT_BLOCK), 1)
    band = (kj > qi) & (kj <= qi + ATT_BLOCK)
    kv_lane0 = _iota((2 * ATT_BLOCK, LANES), 1) < HEAD
    out_lane0 = _iota((ATT_BLOCK, LANES), 1) < HEAD
    n_att_heads = ATT_WIDTH // HEAD
    n_att_blocks = tb // ATT_BLOCK
    k_prev = k_prev_ref[...]
    v_prev = v_prev_ref[...]
    scores, v_ops = [], []
    for blk in range(n_att_blocks):
        rows = slice(blk * ATT_BLOCK, (blk + 1) * ATT_BLOCK)
        k_cat = cat([k_prev, k_att[rows]], axis=0)
        v_cat = cat([v_prev, v_att[rows]], axis=0)
        k_swap = pltpu.roll(k_cat, HEAD, 1)
        v_swap = pltpu.roll(v_cat, HEAD, 1)
        k_ops, v_blk = {}, {}
        for kv_head in range(ATT_KV_WIDTH // HEAD):
            for parity in range(2):
                own_lanes = kv_lane0 if parity == 0 else ~kv_lane0
                k_ops[kv_head, parity] = jnp.where(own_lanes, k_cat if kv_head == parity else k_swap, 0.0)
                v_blk[kv_head, parity] = jnp.where(own_lanes, v_cat if kv_head == parity else v_swap, 1.0)
        for hd in range(n_att_heads):
            kv_head, parity, pair = hd // ATT_GROUP, hd % 2, hd // 2
            scores.append(_mm(q[rows, pair * LANES:(pair + 1) * LANES], k_ops[kv_head, parity], tb=True))
            v_ops.append(v_blk[kv_head, parity])
        k_prev, v_prev = k_att[rows], v_att[rows]
    k_prev_ref[...] = k_prev
    v_prev_ref[...] = v_prev

    deferred.run(tile_steps)

    small = _Deferred()
    outs, sink_terms = {}, {}

    def softmax_pv(i):
        blk, hd = divmod(i, n_att_heads)
        valid = band & ((kj >= ATT_BLOCK) | (t_blk > 0)) if blk == 0 else band
        s = jnp.where(valid, scores[i], NEG_INF)
        sink = sinks_ref[hd]
        m = jnp.maximum(jnp.max(s, axis=-1, keepdims=True), sink)
        outs[i] = _mm(jnp.exp(s - m), v_ops[i])
        sink_terms[i] = jnp.exp(sink - m)

    for i in range(len(scores)):
        small.add(functools.partial(softmax_pv, i), ATT_BLOCK // 16)

    v = token_shift("v")
    k = k * (1.0 + (a - 1.0) * k_a_ref[...])
    bonus = head_sum(r * k * r_k_ref[...])
    kk = kk * jnp.minimum(lax.rsqrt(kk_norm2), 1.0 / L2_EPS)
    alpha = -(a * kk)
    deferred.run(2 * tile_steps)
    slabs, ptots = [], []
    for c in range(n_chunks):
        rows = slice(c * CHUNK, (c + 1) * CHUNK)
        cum = cums[c]
        total = cum[CHUNK - 1:CHUNK, :]
        p_incl = jnp.exp(cum)
        p_inv = jnp.exp(-cum)
        p_tail = jnp.exp(total - cum)
        full = (r[rows] * p_incl, kk[rows] * jnp.exp(cum - log_w[rows]), alpha[rows] * p_inv, k[rows] * p_inv,
                alpha[rows] * p_tail, k[rows] * p_tail, v[rows])
        ptot = jnp.exp(total)
        for gr in range(n_groups):
            ln = slice(gr * SLAB, (gr + 1) * SLAB)
            slabs.append(tuple(z[:, ln] for z in full))
            ptots.append(ptot[:, ln])
    y_slabs = _rwkv_chunks(slabs, ptots, n_groups, state_ref,
                           functools.partial(small.run, len(scores) // 8 * (ATT_BLOCK // 16)),
                           functools.partial(deferred.run, tile_steps))
    y = cat([cat(y_slabs[c * n_groups:(c + 1) * n_groups], axis=1) for c in range(n_chunks)], axis=0)
    small.drain()
    deferred.drain()

    mean = head_sum(y) * (1.0 / HEAD)
    yc = y - mean

    att_blocks = []
    for blk in range(n_att_blocks):
        pair_slabs = []
        for pair in range(ATT_WIDTH // LANES):
            i0 = blk * n_att_heads + 2 * pair
            o0, o1 = outs[i0], outs[i0 + 1]
            num = jnp.where(out_lane0, o0, o1)
            den = pltpu.roll(jnp.where(out_lane0, o1, o0), HEAD, 1)
            den = den + jnp.where(out_lane0, sink_terms[i0], sink_terms[i0 + 1])
            pair_slabs.append(num / den)
        att_blocks.append(cat(pair_slabs, axis=1))
    y_att = cat(att_blocks, axis=0)

    g_att = proj("g_att")
    br_att = _mm(y_att * (g_att * _sigmoid(g_att)), w_br_att_ref[...])
    var = head_sum(yc * yc) * (1.0 / HEAD)
    y = yc * lax.rsqrt(var + GN_EPS) * gn_w_ref[...] + gn_b_ref[...]
    y_rwkv = y + bonus * v
    g_rwkv = proj("g_rwkv")
    br_rwkv = _mm(y_rwkv * (g_rwkv * _sigmoid(g_rwkv)), w_br_rwkv_ref[...])
    merged = _sigmoid(proj("gate_rwkv")) * br_rwkv + _sigmoid(proj("gate_att")) * br_att
    resid = x + _mm(merged, w_out_ref[...])
    out_ref[0] = _rms_norm(resid, g_final_ref[...])


def _const_spec(shape):
    return pl.BlockSpec(shape, lambda b, t: (0,) * len(shape), pipeline_mode=pl.Buffered(1))


def kernel(x, g_pre, w_in, mu_shift, w0, w_decay_up, a0, w_iclr_up, k_k, k_a, r_k, gn_w, gn_b, b_qkv,
           sinks, w_branch_rwkv, w_branch_att, w_out, g_final):
    assert g_pre.shape[0] == 1, "single layer"
    batch, seq, d = x.shape
    assert d == D_MODEL and seq % TOKENS_PER_STEP == 0 and w_in.shape[-1] == IN_COLS
    row = lambda p: p.reshape(1, -1).astype(F32)
    operands = [
        x, row(g_pre[0]), w_in[0],
        row(w0[0]), w_decay_up[0], w_iclr_up[0], row(a0[0]),
        row(k_k[0]), row(k_a[0]), row(r_k[0]), row(gn_w[0]), row(gn_b[0]),
        w_branch_rwkv[0].astype(BF16), w_branch_att[0].astype(BF16), w_out[0].astype(BF16), row(g_final),
        row(mu_shift[0]), row(b_qkv[0]),
    ]
    x_spec = pl.BlockSpec((1, TOKENS_PER_STEP, D_MODEL), lambda b, t: (b, t, 0))
    in_specs = [pl.BlockSpec(memory_space=pltpu.SMEM), x_spec] + [_const_spec(op.shape) for op in operands[1:]]
    in_specs[1 + 2] = pl.BlockSpec(memory_space=pl.ANY)
    return pl.pallas_call(
        _layer_kernel,
        grid=(batch, seq // TOKENS_PER_STEP),
        in_specs=in_specs,
        out_specs=x_spec,
        out_shape=jax.ShapeDtypeStruct(x.shape, x.dtype),
        scratch_shapes=[
            pltpu.VMEM((8, SHIFT_WIDTH), F32),
            pltpu.VMEM((D_MODEL, IN_COLS), BF16),
            pltpu.VMEM((2, W_IN_COPY_ROWS, IN_COLS), F32),
            pltpu.SemaphoreType.DMA((2,)),
            pltpu.VMEM((RWKV_WIDTH // SLAB, SLAB, SLAB), F32),
            pltpu.VMEM((ATT_BLOCK, ATT_KV_WIDTH), F32),
            pltpu.VMEM((ATT_BLOCK, ATT_KV_WIDTH), F32),
        ],
        compiler_params=pltpu.CompilerParams(
            dimension_semantics=("arbitrary", "arbitrary"),
            vmem_limit_bytes=VMEM_LIMIT_BYTES,
        ),
        name="hybrid_rwkv7_swa_layer",
    )(sinks[0].astype(F32), *operands)
```
